```python
import jax, jax.numpy as jnp
from jax import lax
import numpy as np

D_MODEL = 1024
BATCH = 8
SEQ = 4096
DEPTH = 4

N_MIXERS = 2
N_MAMBA = (DEPTH + 1) // 2
N_HGRN = DEPTH // 2
EPS = 1e-6
D_FF = 2816
M_EXPAND = 2
M_INNER = M_EXPAND * D_MODEL
M_HEADDIM = 64
M_HEADS = M_INNER // M_HEADDIM
M_GROUPS = 4
M_STATE = 128
M_CONV = 4
M_CHUNK = 128
M_CONV_DIM = M_INNER + 2 * M_GROUPS * M_STATE
M_PROJ = 2 * M_INNER + 2 * M_GROUPS * M_STATE + M_HEADS
H_EXPAND = 128
H_HEADS = D_MODEL // H_EXPAND
H_KEY = H_HEADS * H_EXPAND
H_VAL = D_MODEL
H_HEAD_V = H_VAL // H_HEADS
H_CHUNK = 16
H_PROJ = 2 * H_KEY + 2 * H_VAL

kernel_name = 'hybrid_mamba2_hgrn2_macaron'


def rmsnorm(x, w):
    xf = x.astype(jnp.float32)
    xf = xf * lax.rsqrt(jnp.mean(xf * xf, axis=-1, keepdims=True) + EPS)
    return (xf * w.astype(jnp.float32)).astype(x.dtype)


def swiglu(u, w_gate, w_up, w_down):
    return (jax.nn.silu(u @ w_gate) * (u @ w_up)) @ w_down


def causal_dwconv(x, w, b):
    y = lax.conv_general_dilated(
        x, w[:, None, :].astype(x.dtype), window_strides=(1,),
        padding=[(M_CONV - 1, 0)], dimension_numbers=('NWC', 'WIO', 'NWC'),
        feature_group_count=x.shape[-1])
    return y + b.astype(x.dtype)


def ssd_chunked(xh, dA, Bm, Cm):
    b, l, h, p = xh.shape
    g, n = Bm.shape[-2:]
    r = h // g
    c = l // M_CHUNK
    X = xh.reshape(b, c, M_CHUNK, g, r, p)
    Bc = Bm.reshape(b, c, M_CHUNK, g, n)
    Cc = Cm.reshape(b, c, M_CHUNK, g, n)
    dA = dA.reshape(b, c, M_CHUNK, g, r).transpose(0, 3, 4, 1, 2)
    cs = jnp.cumsum(dA, axis=-1)
    tril = jnp.tril(jnp.ones((M_CHUNK, M_CHUNK), dtype=bool))
    seg = cs[..., :, None] - cs[..., None, :]
    Lmat = jnp.exp(jnp.where(tril, seg, -jnp.inf))
    CB = jnp.einsum('bclgn,bcsgn->bcgls', Cc, Bc)
    y_diag = jnp.einsum('bcgls,bgrcls,bcsgrp->bclgrp', CB, Lmat, X)
    decay_states = jnp.exp(cs[..., -1:] - cs)
    states = jnp.einsum('bcsgn,bgrcs,bcsgrp->bcgrpn', Bc, decay_states, X)
    chunk_decay = jnp.exp(cs[..., -1])

    def step(hstate, inp):
        st, dec = inp
        return dec[..., None, None] * hstate + st, hstate

    h0 = jnp.zeros((b, g, r, p, n), dtype=X.dtype)
    _, prev = lax.scan(step, h0, (states.transpose(1, 0, 2, 3, 4, 5), chunk_decay.transpose(3, 0, 1, 2)))
    y_off = jnp.einsum('bclgn,cbgrpn,bgrcl->bclgrp', Cc, prev, jnp.exp(cs))
    return (y_diag + y_off).reshape(b, l, h, p)


def mamba2_mixer(u, w_in, conv_w, conv_b, dt_bias, a_log, d_skip, norm_w, w_out):
    b, l, _ = u.shape
    zxbcdt = u @ w_in
    z, xbc, dt = jnp.split(zxbcdt, [M_INNER, M_INNER + M_CONV_DIM], axis=-1)
    xbc = jax.nn.silu(causal_dwconv(xbc, conv_w, conv_b)).astype(jnp.float32)
    xs, Bm, Cm = jnp.split(xbc, [M_INNER, M_INNER + M_GROUPS * M_STATE], axis=-1)
    dt = jax.nn.softplus(dt.astype(jnp.float32) + dt_bias.astype(jnp.float32))
    A = -jnp.exp(a_log.astype(jnp.float32))
    xh = xs.reshape(b, l, M_HEADS, M_HEADDIM)
    y = ssd_chunked(xh * dt[..., None], dt * A,
                    Bm.reshape(b, l, M_GROUPS, M_STATE), Cm.reshape(b, l, M_GROUPS, M_STATE))
    y = y + d_skip.astype(jnp.float32)[:, None] * xh
    yg = (y.reshape(b, l, M_INNER) * jax.nn.silu(z.astype(jnp.float32))).reshape(b, l, M_GROUPS, -1)
    yg = yg * lax.rsqrt(jnp.mean(yg * yg, axis=-1, keepdims=True) + EPS)
    y = yg.reshape(b, l, M_INNER) * norm_w.astype(jnp.float32)
    return y.astype(u.dtype) @ w_out


def hgrn2_chunked(q, k, v, logf):
    b, l, h, dk = q.shape
    dv = v.shape[-1]
    c = l // H_CHUNK

    def to_chunks(t):
        return t.reshape(b, c, H_CHUNK, h, t.shape[-1]).transpose(1, 0, 3, 2, 4)

    tril = jnp.tril(jnp.ones((H_CHUNK, H_CHUNK), dtype=bool))[:, :, None]

    def step(S, inp):
        qc, kc, vc, gc = inp
        cs = jnp.cumsum(gc, axis=-2)
        o_inter = jnp.einsum('bhld,bhdv->bhlv', qc * jnp.exp(cs), S)
        decay = jnp.exp(jnp.where(tril, cs[:, :, :, None, :] - cs[:, :, None, :, :], -jnp.inf))
        attn = jnp.einsum('bhld,bhlsd,bhsd->bhls', qc, decay, kc)
        o = o_inter + jnp.einsum('bhls,bhsv->bhlv', attn, vc)
        cs_last = cs[:, :, -1:, :]
        S_new = jnp.exp(cs_last[:, :, 0, :])[..., None] * S + jnp.einsum(
            'bhsd,bhsv->bhdv', kc * jnp.exp(cs_last - cs), vc)
        return S_new, o

    S0 = jnp.zeros((b, h, dk, dv), dtype=q.dtype)
    _, o = lax.scan(step, S0, (to_chunks(q), to_chunks(k), to_chunks(v), to_chunks(logf)))
    return o.transpose(1, 0, 3, 2, 4).reshape(b, l, h, dv)


def hgrn2_mixer(u, w_in, lower_bound, norm_w, w_out):
    b, l, _ = u.shape
    q, f, v, g = jnp.split((u @ w_in).astype(jnp.float32), [H_KEY, 2 * H_KEY, 2 * H_KEY + H_VAL], axis=-1)
    q = jax.nn.silu(q)
    forget = lower_bound + (1.0 - lower_bound) * jax.nn.sigmoid(f)
    k = 1.0 - forget
    logf = jnp.log(forget)
    o = hgrn2_chunked(q.reshape(b, l, H_HEADS, H_EXPAND), k.reshape(b, l, H_HEADS, H_EXPAND),
                      v.reshape(b, l, H_HEADS, H_HEAD_V), logf.reshape(b, l, H_HEADS, H_EXPAND))
    o = o * lax.rsqrt(jnp.mean(o * o, axis=-1, keepdims=True) + EPS) * norm_w.astype(jnp.float32)
    o = o.reshape(b, l, H_VAL) * jax.nn.silu(g)
    return o.astype(u.dtype) @ w_out


def setup_inputs(seed: int = 0) -> dict:
    key = jax.random.key(seed)
    ks = jax.random.split(key, 20)
    nrm = jax.random.normal
    x = nrm(ks[0], (BATCH, SEQ, D_MODEL), jnp.float32)
    norm_w = 1.0 + 0.02 * nrm(ks[1], (DEPTH, 3, D_MODEL), jnp.float32)
    ffn_w_gate = nrm(ks[2], (DEPTH, 2, D_MODEL, D_FF), jnp.float32) * D_MODEL ** -0.5
    ffn_w_up = nrm(ks[3], (DEPTH, 2, D_MODEL, D_FF), jnp.float32) * D_MODEL ** -0.5
    ffn_w_down = nrm(ks[4], (DEPTH, 2, D_FF, D_MODEL), jnp.float32) * D_FF ** -0.5
    m_w_in = nrm(ks[5], (N_MAMBA, D_MODEL, M_PROJ), jnp.float32) * D_MODEL ** -0.5
    m_conv_w = nrm(ks[6], (N_MAMBA, M_CONV, M_CONV_DIM), jnp.float32) * M_CONV ** -0.5
    m_conv_b = 0.01 * nrm(ks[7], (N_MAMBA, M_CONV_DIM), jnp.float32)
    u = jax.random.uniform(ks[8], (N_MAMBA, M_HEADS), jnp.float32)
    dt0 = jnp.exp(u * (jnp.log(0.1) - jnp.log(0.001)) + jnp.log(0.001))
    m_dt_bias = dt0 + jnp.log(-jnp.expm1(-dt0))
    m_a_log = jnp.log(jax.random.uniform(ks[9], (N_MAMBA, M_HEADS), jnp.float32, 1.0, 16.0))
    m_d = 1.0 + 0.02 * nrm(ks[10], (N_MAMBA, M_HEADS), jnp.float32)
    m_norm_w = 1.0 + 0.02 * nrm(ks[11], (N_MAMBA, M_INNER), jnp.float32)
    m_w_out = nrm(ks[12], (N_MAMBA, M_INNER, D_MODEL), jnp.float32) * M_INNER ** -0.5
    h_w_in = nrm(ks[13], (N_HGRN, D_MODEL, H_PROJ), jnp.float32) * D_MODEL ** -0.5
    h_lb_logits = 0.1 * nrm(ks[14], (DEPTH, H_KEY), jnp.float32)
    h_norm_w = 1.0 + 0.02 * nrm(ks[15], (N_HGRN, H_HEAD_V), jnp.float32)
    h_w_out = nrm(ks[16], (N_HGRN, H_VAL, D_MODEL), jnp.float32) * H_VAL ** -0.5
    final_norm_w = 1.0 + 0.02 * nrm(ks[17], (D_MODEL,), jnp.float32)
    return {'x': x, 'norm_w': norm_w, 'ffn_w_gate': ffn_w_gate, 'ffn_w_up': ffn_w_up,
            'ffn_w_down': ffn_w_down, 'm_w_in': m_w_in, 'm_conv_w': m_conv_w, 'm_conv_b': m_conv_b,
            'm_dt_bias': m_dt_bias, 'm_a_log': m_a_log, 'm_d': m_d, 'm_norm_w': m_norm_w,
            'm_w_out': m_w_out, 'h_w_in': h_w_in, 'h_lb_logits': h_lb_logits, 'h_norm_w': h_norm_w,
            'h_w_out': h_w_out, 'final_norm_w': final_norm_w}


def reference(x, norm_w, ffn_w_gate, ffn_w_up, ffn_w_down, m_w_in, m_conv_w, m_conv_b,
              m_dt_bias, m_a_log, m_d, m_norm_w, m_w_out, h_w_in, h_lb_logits, h_norm_w,
              h_w_out, final_norm_w):
    s = jax.nn.softmax(h_lb_logits.astype(jnp.float32), axis=0)
    lower_bounds = jnp.cumsum(s, axis=0) - s[0]
    for i in range(DEPTH):
        x = x + 0.5 * swiglu(rmsnorm(x, norm_w[i, 0]), ffn_w_gate[i, 0], ffn_w_up[i, 0], ffn_w_down[i, 0])
        u = rmsnorm(x, norm_w[i, 1])
        j = i // N_MIXERS
        if i % N_MIXERS == 0:
            mix = mamba2_mixer(u, m_w_in[j], m_conv_w[j], m_conv_b[j], m_dt_bias[j], m_a_log[j],
                               m_d[j], m_norm_w[j], m_w_out[j])
        else:
            mix = hgrn2_mixer(u, h_w_in[j], lower_bounds[i], h_norm_w[j], h_w_out[j])
        x = x + mix
        x = x + 0.5 * swiglu(rmsnorm(x, norm_w[i, 2]), ffn_w_gate[i, 1], ffn_w_up[i, 1], ffn_w_down[i, 1])
    return rmsnorm(x, final_norm_w)
```

```python
import functools

import jax
import jax.numpy as jnp
from jax import lax
from jax.experimental import pallas as pl
from jax.experimental.pallas import tpu as pltpu

EPS = 1e-6
F32 = jnp.float32
BF16 = jnp.bfloat16

M_HEADDIM = 64
M_GROUPS = 4
M_STATE = 128
M_CONV = 4
H_HEAD = 128

CHUNK = 128
SUBLANES = 8
VMEM_LIMIT = 56 * 1024 * 1024


def _dot(a, b):
    return jnp.dot(a, b, preferred_element_type=F32)


def _dot_nt(a, b):
    return lax.dot_general(a, b, (((1,), (1,)), ((), ())), preferred_element_type=F32)


def _rms(x, w):
    return x * lax.rsqrt(jnp.mean(x * x, axis=-1, keepdims=True) + EPS) * w


def _silu(x):
    return x * jax.nn.sigmoid(x)


def _cumsum_rows(x, tril_bf):
    hi = x.astype(BF16)
    r1 = x - hi.astype(F32)
    mid = r1.astype(BF16)
    lo = (r1 - mid.astype(F32)).astype(BF16)
    return _dot(tril_bf, hi) + _dot(tril_bf, mid) + _dot(tril_bf, lo)


def _const_spec(shape):
    nd = len(shape)
    return pl.BlockSpec(shape, lambda *_: (0,) * nd, pipeline_mode=pl.Buffered(1))


def _ffn_kernel(x_ref, nw_ref, wg_ref, wu_ref, wd_ref, fnw_ref, o_ref, h_ref, *, f_chunk, final):
    x = x_ref[...]
    xn = _rms(x, nw_ref[...]).astype(BF16)
    d_ff = wg_ref.shape[1]
    for c in range(d_ff // f_chunk):
        sl = slice(c * f_chunk, (c + 1) * f_chunk)
        g = _dot(xn, wg_ref[:, sl])
        u = _dot(xn, wu_ref[:, sl])
        h_ref[:, sl] = (_silu(g) * u).astype(BF16)
    y = x + 0.5 * _dot(h_ref[...], wd_ref[...])
    if final:
        y = _rms(y, fnw_ref[...])
    o_ref[...] = y


def _ffn(x2d, nw, wg, wu, wd, fnw, *, final, tm=512, f_chunk=256):
    t, d = x2d.shape
    d_ff = wg.shape[1]
    return pl.pallas_call(
        functools.partial(_ffn_kernel, f_chunk=f_chunk, final=final),
        out_shape=jax.ShapeDtypeStruct((t, d), F32),
        grid=(t // tm,),
        in_specs=[
            pl.BlockSpec((tm, d), lambda i: (i, 0)),
            _const_spec((1, d)),
            _const_spec((d, d_ff)),
            _const_spec((d, d_ff)),
            _const_spec((d_ff, d)),
            _const_spec((1, d)),
        ],
        out_specs=pl.BlockSpec((tm, d), lambda i: (i, 0)),
        scratch_shapes=[pltpu.VMEM((tm, d_ff), BF16)],
        compiler_params=pltpu.CompilerParams(
            dimension_semantics=("arbitrary",), vmem_limit_bytes=VMEM_LIMIT),
        name="ffn_final" if final else "ffn",
    )(x2d, nw, wg, wu, wd, fnw)


def _lane_bcast(col):
    return jnp.broadcast_to(col, (col.shape[0], 128))


def _mamba_kernel(x_ref, nw_ref, wz_ref, wxbc_ref, wdt_ref, cw_ref, cb_ref, dtb_ref, alog_ref,
                  dexp_ref, gnw_ref, wout_ref, o_ref,
                  z_ref, xbc_ref, xc_ref, dt_ref, ybf_ref, st_ref, *, tl):
    d_inner = z_ref.shape[1]
    n_pairs = d_inner // 128
    pairs_per_group = n_pairs // M_GROUPS
    gwidth = d_inner // M_GROUPS
    b_off = d_inner
    c_off = d_inner + M_GROUPS * M_STATE

    @pl.when(pl.program_id(1) == 0)
    def _():
        st_ref[...] = jnp.zeros_like(st_ref)
        xbc_ref[0:SUBLANES, :] = jnp.zeros((SUBLANES, xbc_ref.shape[1]), F32)

    x = x_ref[...]
    u = _rms(x, nw_ref[...]).astype(BF16)
    z_ref[...] = _dot(u, wz_ref[...])
    xbc_ref[SUBLANES:SUBLANES + tl, :] = _dot(u, wxbc_ref[...])
    dt_ref[...] = _dot(u, wdt_ref[...])

    for r in range(tl // CHUNK):
        base = SUBLANES + r * CHUNK
        acc = cb_ref[...] + cw_ref[M_CONV - 1:M_CONV, :] * xbc_ref[base:base + CHUNK, :]
        for k in range(M_CONV - 1):
            off = base - (M_CONV - 1) + k
            acc = acc + cw_ref[k:k + 1, :] * xbc_ref[off:off + CHUNK, :]
        xc_ref[r * CHUNK:(r + 1) * CHUNK, :] = _silu(acc)
    xbc_ref[0:SUBLANES, :] = xbc_ref[tl:tl + SUBLANES, :]

    a_neg = -jnp.exp(alog_ref[...])
    row = lax.broadcasted_iota(jnp.int32, (CHUNK, CHUNK), 0)
    col = lax.broadcasted_iota(jnp.int32, (CHUNK, CHUNK), 1)
    tril = row >= col
    tril_bf = jnp.where(tril, 1.0, 0.0).astype(BF16)
    first_head = col < M_HEADDIM

    def pair_select(a, h0):
        return jnp.where(first_head, _lane_bcast(a[:, h0:h0 + 1]), _lane_bcast(a[:, h0 + 1:h0 + 2]))

    def chunk_body(c, carry):
        r0 = pl.multiple_of(c * CHUNK, CHUNK)
        rows = pl.ds(r0, CHUNK)
        dt_raw = dt_ref[rows, :] + dtb_ref[...]
        dt = jnp.maximum(dt_raw, 0.0) + jnp.log1p(jnp.exp(-jnp.abs(dt_raw)))
        cs = _cumsum_rows(dt * a_neg, tril_bf)
        cs_t = cs.T
        dt_t = dt.T
        cs_last = cs[CHUNK - 1:CHUNK, :]
        w_state = jnp.exp(cs_last - cs) * dt
        chunk_decay = jnp.exp(cs_last)
        ecs = jnp.exp(cs)

        for g in range(M_GROUPS):
            bg = xc_ref[rows, b_off + g * M_STATE:b_off + (g + 1) * M_STATE]
            cg = xc_ref[rows, c_off + g * M_STATE:c_off + (g + 1) * M_STATE]
            bg_bf = bg.astype(BF16)
            cg_bf = cg.astype(BF16)
            cb = _dot_nt(cg_bf, bg_bf)
            gcols = slice(g * gwidth, (g + 1) * gwidth)
            sg = st_ref[:, gcols]
            y_off = _dot(cg_bf, sg.astype(BF16))
            y_parts, xw_parts, cd_parts = [], [], []
            for jj in range(pairs_per_group):
                j = g * pairs_per_group + jj
                h0 = 2 * j
                pcols = slice(j * 128, (j + 1) * 128)
                xs = xc_ref[rows, pcols]
                xs_bf = xs.astype(BF16)
                yd = []
                for h in (h0, h0 + 1):
                    seg = _lane_bcast(cs[:, h:h + 1]) - cs_t[h:h + 1, :]
                    lmat = jnp.exp(jnp.where(tril, seg, -jnp.inf))
                    m = (cb * lmat * dt_t[h:h + 1, :]).astype(BF16)
                    yd.append(_dot(m, xs_bf))
                y = jnp.where(first_head, yd[0], yd[1])
                y = y + pair_select(ecs, h0) * y_off[:, jj * 128:(jj + 1) * 128]
                y = y + dexp_ref[:, pcols] * xs
                y_parts.append(y)
                xw_parts.append((pair_select(w_state, h0) * xs).astype(BF16))
                cd_parts.append(pair_select(chunk_decay, h0))
            xw = jnp.concatenate(xw_parts, axis=1)
            cd = jnp.concatenate(cd_parts, axis=1)
            st_ref[:, gcols] = cd * sg + _dot(bg.T.astype(BF16), xw)
            yg = jnp.concatenate(y_parts, axis=1) * _silu(z_ref[rows, gcols])
            yg = yg * lax.rsqrt(jnp.mean(yg * yg, axis=-1, keepdims=True) + EPS)
            ybf_ref[rows, gcols] = (yg * gnw_ref[:, gcols]).astype(BF16)
        return carry

    lax.fori_loop(0, tl // CHUNK, chunk_body, 0)
    o_ref[...] = x + _dot(ybf_ref[...], wout_ref[...])


def _mamba(x, nw, w_in, conv_w, conv_b, dt_bias, a_log, d_skip, gnw, w_out, *, tl=512):
    b, l, d = x.shape
    n_heads = dt_bias.shape[0]
    d_inner = n_heads * M_HEADDIM
    conv_dim = d_inner + 2 * M_GROUPS * M_STATE
    pad = 128 - n_heads
    wz = w_in[:, :d_inner].astype(BF16)
    wxbc = w_in[:, d_inner:d_inner + conv_dim].astype(BF16)
    wdt = jnp.pad(w_in[:, d_inner + conv_dim:], ((0, 0), (0, pad))).astype(BF16)
    dtb = jnp.pad(dt_bias, (0, pad)).reshape(1, 128)
    alog = jnp.pad(a_log, (0, pad)).reshape(1, 128)
    dexp = jnp.repeat(d_skip, M_HEADDIM).reshape(1, d_inner)
    return pl.pallas_call(
        functools.partial(_mamba_kernel, tl=tl),
        out_shape=jax.ShapeDtypeStruct((b, l, d), F32),
        grid=(b, l // tl),
        in_specs=[
            pl.BlockSpec((None, tl, d), lambda i, j: (i, j, 0)),
            _const_spec((1, d)),
            _const_spec((d, d_inner)),
            _const_spec((d, conv_dim)),
            _const_spec((d, 128)),
            _const_spec((M_CONV, conv_dim)),
            _const_spec((1, conv_dim)),
            _const_spec((1, 128)),
            _const_spec((1, 128)),
            _const_spec((1, d_inner)),
            _const_spec((1, d_inner)),
            _const_spec((d_inner, d)),
        ],
        out_specs=pl.BlockSpec((None, tl, d), lambda i, j: (i, j, 0)),
        scratch_shapes=[
            pltpu.VMEM((tl, d_inner), F32),
            pltpu.VMEM((SUBLANES + tl, conv_dim), F32),
            pltpu.VMEM((tl, conv_dim), F32),
            pltpu.VMEM((tl, 128), F32),
            pltpu.VMEM((tl, d_inner), BF16),
            pltpu.VMEM((M_STATE, d_inner), F32),
        ],
        compiler_params=pltpu.CompilerParams(
            dimension_semantics=("arbitrary", "arbitrary"), vmem_limit_bytes=VMEM_LIMIT),
        name="mamba",
    )(x, nw.reshape(1, d), wz, wxbc, wdt, conv_w, conv_b.reshape(1, conv_dim), dtb, alog, dexp,
      gnw.reshape(1, d_inner), w_out.astype(BF16))


def _block_reference(p, b):
    n, w = p.shape
    span = 2 * b
    if span >= 2 * SUBLANES:
        parts = [jnp.broadcast_to(p[i * span + b - 1:i * span + b, :], (span, w)) for i in range(n // span)]
        return jnp.concatenate(parts, axis=0)
    p3 = p.reshape(n // SUBLANES, SUBLANES, w)
    sub = lax.broadcasted_iota(jnp.int32, p3.shape, 1)
    out = None
    for i in range(SUBLANES // span):
        m = i * span + b - 1
        piece = jnp.broadcast_to(p3[:, m:m + 1, :], p3.shape)
        out = piece if out is None else jnp.where(sub >= i * span, piece, out)
    return out.reshape(n, w)


def _hgrn_kernel(x_ref, nw_ref, win_ref, lbl_ref, hnw_ref, wout_ref, o_ref,
                 proj_ref, obf_ref, st_ref, *, tl, layer):
    d_key = lbl_ref.shape[1]
    n_heads = d_key // H_HEAD
    f_off, v_off, g_off = d_key, 2 * d_key, 2 * d_key + n_heads * H_HEAD

    @pl.when(pl.program_id(1) == 0)
    def _():
        st_ref[...] = jnp.zeros_like(st_ref)

    x = x_ref[...]
    u = _rms(x, nw_ref[...]).astype(BF16)
    proj_ref[...] = _dot(u, win_ref[...])

    logits = lbl_ref[...]
    e = jnp.exp(logits - jnp.max(logits, axis=0, keepdims=True))
    lb = jnp.sum(e[1:layer + 1, :], axis=0, keepdims=True) / jnp.sum(e, axis=0, keepdims=True)

    row = lax.broadcasted_iota(jnp.int32, (CHUNK, CHUNK), 0)
    col = lax.broadcasted_iota(jnp.int32, (CHUNK, CHUNK), 1)
    tril_bf = jnp.where(row >= col, 1.0, 0.0).astype(BF16)
    levels = []
    k = 0
    while (1 << k) < CHUNK:
        right = ((row >> k) & 1) == 1
        pair = right & (((col >> k) & 1) == 0) & ((row >> (k + 1)) == (col >> (k + 1)))
        levels.append((k, right, pair))
        k += 1

    def chunk_body(c, carry):
        r0 = pl.multiple_of(c * CHUNK, CHUNK)
        rows = pl.ds(r0, CHUNK)
        forget = lb + (1.0 - lb) * jax.nn.sigmoid(proj_ref[rows, f_off:f_off + d_key])
        logf = jnp.log(forget)
        p_all = _cumsum_rows(logf, tril_bf)
        for h in range(n_heads):
            hc = slice(h * H_HEAD, (h + 1) * H_HEAD)
            q = _silu(proj_ref[rows, h * H_HEAD:(h + 1) * H_HEAD])
            fg = forget[:, hc]
            kk = 1.0 - fg
            v_bf = proj_ref[rows, v_off + h * H_HEAD:v_off + (h + 1) * H_HEAD].astype(BF16)
            p = p_all[:, hc]
            s_in = st_ref[h]
            o = _dot((q * jnp.exp(p)).astype(BF16), s_in.astype(BF16))
            kk_bf = kk.astype(BF16)
            a = jnp.where(row == col, _dot_nt(q.astype(BF16), kk_bf), 0.0)
            for (lg, right, pair) in levels:
                if lg == 0:
                    qt = q * jnp.where(right, fg, 1.0)
                    kt_bf = kk_bf
                else:
                    dist = jnp.exp(-jnp.abs(p - _block_reference(p, 1 << lg)))
                    qt = q * jnp.where(right, dist, 1.0)
                    kt_bf = (kk * jnp.where(right, 1.0, dist)).astype(BF16)
                a = jnp.where(pair, _dot_nt(qt.astype(BF16), kt_bf), a)
            o = o + _dot(a.astype(BF16), v_bf)
            p_last = p[CHUNK - 1:CHUNK, :]
            k_end = kk * jnp.exp(p_last - p)
            decay_col = jnp.broadcast_to(jnp.exp(p_last), (CHUNK, H_HEAD)).T
            st_ref[h] = decay_col * s_in + _dot(k_end.T.astype(BF16), v_bf)
            o = o * lax.rsqrt(jnp.mean(o * o, axis=-1, keepdims=True) + EPS) * hnw_ref[...]
            o = o * _silu(proj_ref[rows, g_off + h * H_HEAD:g_off + (h + 1) * H_HEAD])
            obf_ref[rows, hc] = o.astype(BF16)
        return carry

    lax.fori_loop(0, tl // CHUNK, chunk_body, 0)
    o_ref[...] = x + _dot(obf_ref[...], wout_ref[...])


def _hgrn(x, nw, w_in, lb_logits, hnw, w_out, *, layer, tl=512):
    b, l, d = x.shape
    depth, d_key = lb_logits.shape
    d_proj = w_in.shape[1]
    d_val = w_out.shape[0]
    n_heads = d_key // H_HEAD
    return pl.pallas_call(
        functools.partial(_hgrn_kernel, tl=tl, layer=layer),
        out_shape=jax.ShapeDtypeStruct((b, l, d), F32),
        grid=(b, l // tl),
        in_specs=[
            pl.BlockSpec((None, tl, d), lambda i, j: (i, j, 0)),
            _const_spec((1, d)),
            _const_spec((d, d_proj)),
            _const_spec((depth, d_key)),
            _const_spec((1, H_HEAD)),
            _const_spec((d_val, d)),
        ],
        out_specs=pl.BlockSpec((None, tl, d), lambda i, j: (i, j, 0)),
        scratch_shapes=[
            pltpu.VMEM((tl, d_proj), F32),
            pltpu.VMEM((tl, d_val), BF16),
            pltpu.VMEM((n_heads, H_HEAD, H_HEAD), F32),
        ],
        compiler_params=pltpu.CompilerParams(
            dimension_semantics=("arbitrary", "arbitrary"), vmem_limit_bytes=VMEM_LIMIT),
        name="hgrn",
    )(x, nw.reshape(1, d), w_in.astype(BF16), lb_logits, hnw.reshape(1, H_HEAD), w_out.astype(BF16))


def kernel(x, norm_w, ffn_w_gate, ffn_w_up, ffn_w_down, m_w_in, m_conv_w, m_conv_b, m_dt_bias, m_a_log,
           m_d, m_norm_w, m_w_out, h_w_in, h_lb_logits, h_norm_w, h_w_out, final_norm_w):
    b, l, d = x.shape
    depth = norm_w.shape[0]
    fnw = final_norm_w.reshape(1, d)

    def ffn(x, i, j, final=False):
        y = _ffn(x.reshape(b * l, d), norm_w[i, 2 * j].reshape(1, d), ffn_w_gate[i, j].astype(BF16),
                 ffn_w_up[i, j].astype(BF16), ffn_w_down[i, j].astype(BF16), fnw, final=final)
        return y.reshape(b, l, d)

    for i in range(depth):
        x = ffn(x, i, 0)
        j = i // 2
        if i % 2 == 0:
            x = _mamba(x, norm_w[i, 1], m_w_in[j], m_conv_w[j], m_conv_b[j], m_dt_bias[j], m_a_log[j],
                       m_d[j], m_norm_w[j], m_w_out[j])
        else:
            x = _hgrn(x, norm_w[i, 1], h_w_in[j], h_lb_logits, h_norm_w[j], h_w_out[j], layer=i)
        x = ffn(x, i, 1, final=(i == depth - 1))
    return x
```

```python
import functools

import jax
import jax.numpy as jnp
from jax import lax
from jax.experimental import pallas as pl
from jax.experimental.pallas import tpu as pltpu

EPS = 1e-6
LOG2E = 1.4426950408889634
F32 = jnp.float32
BF16 = jnp.bfloat16

M_HEADDIM = 64
M_GROUPS = 4
M_STATE = 128
M_CONV = 4
H_HEAD = 128

CHUNK = 128
SUBLANES = 8
COL_BLOCK = 256
VMEM_LIMIT = 56 * 1024 * 1024


def _dot(a, b):
    return jnp.dot(a, b, preferred_element_type=F32)


def _dot_nt(a, b):
    return lax.dot_general(a, b, (((1,), (1,)), ((), ())), preferred_element_type=F32)


def _rms(x, w):
    return x * lax.rsqrt(jnp.mean(x * x, axis=-1, keepdims=True) + EPS) * w


def _silu(x):
    return x * jax.nn.sigmoid(x)


def _cumsum_rows(x, tril_bf):
    hi = x.astype(BF16)
    r1 = x - hi.astype(F32)
    mid = r1.astype(BF16)
    lo = (r1 - mid.astype(F32)).astype(BF16)
    return _dot(tril_bf, hi) + _dot(tril_bf, mid) + _dot(tril_bf, lo)


def _const_spec(shape):
    nd = len(shape)
    return pl.BlockSpec(shape, lambda *_: (0,) * nd, pipeline_mode=pl.Buffered(1))


def _ffn_kernel(x_ref, nw_ref, wg_ref, wu_ref, wd_ref, fnw_ref, o_ref, h_ref, *, f_chunk, final):
    x = x_ref[...]
    xn = _rms(x, nw_ref[...]).astype(BF16)
    d_ff = wg_ref.shape[1]
    for c in range(d_ff // f_chunk):
        sl = slice(c * f_chunk, (c + 1) * f_chunk)
        g = _dot(xn, wg_ref[:, sl])
        u = _dot(xn, wu_ref[:, sl])
        h_ref[:, sl] = (_silu(g) * u).astype(BF16)
    y = x + 0.5 * _dot(h_ref[...], wd_ref[...])
    if final:
        y = _rms(y, fnw_ref[...])
    o_ref[...] = y


def _ffn(x2d, nw, wg, wu, wd, fnw, *, final, tm=512, f_chunk=256):
    t, d = x2d.shape
    d_ff = wg.shape[1]
    return pl.pallas_call(
        functools.partial(_ffn_kernel, f_chunk=f_chunk, final=final),
        out_shape=jax.ShapeDtypeStruct((t, d), F32),
        grid=(t // tm,),
        in_specs=[
            pl.BlockSpec((tm, d), lambda i: (i, 0)),
            _const_spec((1, d)),
            _const_spec((d, d_ff)),
            _const_spec((d, d_ff)),
            _const_spec((d_ff, d)),
            _const_spec((1, d)),
        ],
        out_specs=pl.BlockSpec((tm, d), lambda i: (i, 0)),
        scratch_shapes=[pltpu.VMEM((tm, d_ff), BF16)],
        compiler_params=pltpu.CompilerParams(
            dimension_semantics=("arbitrary",), vmem_limit_bytes=VMEM_LIMIT),
        name="ffn_final" if final else "ffn",
    )(x2d, nw, wg, wu, wd, fnw)


def _lane_bcast(col):
    return jnp.broadcast_to(col, (col.shape[0], 128))


def _mamba_kernel(x_ref, nw_ref, wz_ref, wxbc_ref, wdt_ref, cw_ref, cb_ref, dtb_ref, alog_ref,
                  dexp_ref, gnw_ref, wout_ref, o_ref,
                  z_ref, xbc_ref, xc_ref, dt_ref, ybf_ref, st_ref, *, tl):
    d_inner = z_ref.shape[1]
    n_pairs = d_inner // 128
    pairs_per_group = n_pairs // M_GROUPS
    gwidth = d_inner // M_GROUPS
    b_off = d_inner
    c_off = d_inner + M_GROUPS * M_STATE

    @pl.when(pl.program_id(1) == 0)
    def _():
        st_ref[...] = jnp.zeros_like(st_ref)
        xbc_ref[0:SUBLANES, :] = jnp.zeros((SUBLANES, xbc_ref.shape[1]), F32)

    x = x_ref[...]
    u = _rms(x, nw_ref[...]).astype(BF16)
    dt_ref[...] = _dot(u, wdt_ref[...])
    n_xbc_blocks = xbc_ref.shape[1] // COL_BLOCK
    n_z_blocks = d_inner // COL_BLOCK
    for j in range(max(n_xbc_blocks, n_z_blocks)):
        cols = slice(j * COL_BLOCK, (j + 1) * COL_BLOCK)
        if j < n_xbc_blocks:
            xbc_ref[SUBLANES:SUBLANES + tl, cols] = _dot(u, wxbc_ref[:, cols])
        if j < n_z_blocks:
            z_ref[:, cols] = _dot(u, wz_ref[:, cols])
        if j < n_xbc_blocks:
            for r in range(tl // CHUNK):
                base = SUBLANES + r * CHUNK
                acc = cb_ref[:, cols] + cw_ref[M_CONV - 1:M_CONV, cols] * xbc_ref[base:base + CHUNK, cols]
                for k in range(M_CONV - 1):
                    off = base - (M_CONV - 1) + k
                    acc = acc + cw_ref[k:k + 1, cols] * xbc_ref[off:off + CHUNK, cols]
                xc_ref[r * CHUNK:(r + 1) * CHUNK, cols] = _silu(acc)
            xbc_ref[0:SUBLANES, cols] = xbc_ref[tl:tl + SUBLANES, cols]

    a_neg2 = -jnp.exp(alog_ref[...]) * LOG2E
    row = lax.broadcasted_iota(jnp.int32, (CHUNK, CHUNK), 0)
    col = lax.broadcasted_iota(jnp.int32, (CHUNK, CHUNK), 1)
    tril = row >= col
    tril_bf = jnp.where(tril, 1.0, 0.0).astype(BF16)
    first_lanes = col < M_HEADDIM
    first_rows = row < M_HEADDIM

    def chunk_body(c, carry):
        r0 = pl.multiple_of(c * CHUNK, CHUNK)
        rows = pl.ds(r0, CHUNK)
        dt_raw = dt_ref[rows, :] + dtb_ref[...]
        dt = jnp.maximum(dt_raw, 0.0) + jnp.log1p(jnp.exp(-jnp.abs(dt_raw)))
        cs = _cumsum_rows(dt * a_neg2, tril_bf)
        cs_t = cs.T
        dt_t = dt.T
        cs_last_t = cs_t[:, CHUNK - 1:CHUNK]
        w_t = jnp.exp2(cs_last_t - cs_t) * dt_t
        cd_t = jnp.exp2(cs_last_t)

        for g in range(M_GROUPS):
            bg_bf = xc_ref[rows, b_off + g * M_STATE:b_off + (g + 1) * M_STATE].astype(BF16)
            cg_bf = xc_ref[rows, c_off + g * M_STATE:c_off + (g + 1) * M_STATE].astype(BF16)
            cb = _dot_nt(cg_bf, bg_bf)
            grows = slice(g * gwidth, (g + 1) * gwidth)
            y_off = _dot_nt(cg_bf, st_ref[grows, :].astype(BF16))
            y_parts = []
            for jj in range(pairs_per_group):
                j = g * pairs_per_group + jj
                h0 = 2 * j
                pcols = slice(j * 128, (j + 1) * 128)
                xs = xc_ref[rows, pcols]
                xs_bf = xs.astype(BF16)
                yd, ecol = [], []
                for h in (h0, h0 + 1):
                    c_col = _lane_bcast(cs[:, h:h + 1])
                    lmat = jnp.exp2(jnp.where(tril, c_col - cs_t[h:h + 1, :], -jnp.inf))
                    m = (cb * lmat * dt_t[h:h + 1, :]).astype(BF16)
                    yd.append(_dot(m, xs_bf))
                    ecol.append(jnp.exp2(c_col))
                y = jnp.where(first_lanes, yd[0], yd[1])
                y = y + jnp.where(first_lanes, ecol[0], ecol[1]) * y_off[:, jj * 128:(jj + 1) * 128]
                y_parts.append(y + dexp_ref[:, pcols] * xs)
                w_rows = jnp.where(first_rows, w_t[h0:h0 + 1, :], w_t[h0 + 1:h0 + 2, :])
                cd_rows = jnp.where(first_rows, cd_t[h0:h0 + 1, :], cd_t[h0 + 1:h0 + 2, :])
                xw_t = (xs.T * w_rows).astype(BF16)
                st_ref[pcols, :] = cd_rows * st_ref[pcols, :] + _dot(xw_t, bg_bf)
            yg = jnp.concatenate(y_parts, axis=1) * _silu(z_ref[rows, grows])
            yg = yg * lax.rsqrt(jnp.mean(yg * yg, axis=-1, keepdims=True) + EPS)
            ybf_ref[rows, grows] = (yg * gnw_ref[:, grows]).astype(BF16)
        return carry

    lax.fori_loop(0, tl // CHUNK, chunk_body, 0)
    o_ref[...] = x + _dot(ybf_ref[...], wout_ref[...])


def _mamba(x, nw, w_in, conv_w, conv_b, dt_bias, a_log, d_skip, gnw, w_out, *, tl=512):
    b, l, d = x.shape
    n_heads = dt_bias.shape[0]
    d_inner = n_heads * M_HEADDIM
    conv_dim = d_inner + 2 * M_GROUPS * M_STATE
    pad = 128 - n_heads
    wz = w_in[:, :d_inner].astype(BF16)
    wxbc = w_in[:, d_inner:d_inner + conv_dim].astype(BF16)
    wdt = jnp.pad(w_in[:, d_inner + conv_dim:], ((0, 0), (0, pad))).astype(BF16)
    dtb = jnp.pad(dt_bias, (0, pad)).reshape(1, 128)
    alog = jnp.pad(a_log, (0, pad)).reshape(1, 128)
    dexp = jnp.repeat(d_skip, M_HEADDIM).reshape(1, d_inner)
    return pl.pallas_call(
        functools.partial(_mamba_kernel, tl=tl),
        out_shape=jax.ShapeDtypeStruct((b, l, d), F32),
        grid=(b, l // tl),
        in_specs=[
            pl.BlockSpec((None, tl, d), lambda i, j: (i, j, 0)),
            _const_spec((1, d)),
            _const_spec((d, d_inner)),
            _const_spec((d, conv_dim)),
            _const_spec((d, 128)),
            _const_spec((M_CONV, conv_dim)),
            _const_spec((1, conv_dim)),
            _const_spec((1, 128)),
            _const_spec((1, 128)),
            _const_spec((1, d_inner)),
            _const_spec((1, d_inner)),
            _const_spec((d_inner, d)),
        ],
        out_specs=pl.BlockSpec((None, tl, d), lambda i, j: (i, j, 0)),
        scratch_shapes=[
            pltpu.VMEM((tl, d_inner), F32),
            pltpu.VMEM((SUBLANES + tl, conv_dim), F32),
            pltpu.VMEM((tl, conv_dim), F32),
            pltpu.VMEM((tl, 128), F32),
            pltpu.VMEM((tl, d_inner), BF16),
            pltpu.VMEM((d_inner, M_STATE), F32),
        ],
        compiler_params=pltpu.CompilerParams(
            dimension_semantics=("arbitrary", "arbitrary"), vmem_limit_bytes=VMEM_LIMIT),
        name="mamba",
    )(x, nw.reshape(1, d), wz, wxbc, wdt, conv_w, conv_b.reshape(1, conv_dim), dtb, alog, dexp,
      gnw.reshape(1, d_inner), w_out.astype(BF16))


def _block_reference(p, b):
    n, w = p.shape
    span = 2 * b
    if span >= 2 * SUBLANES:
        parts = [jnp.broadcast_to(p[i * span + b - 1:i * span + b, :], (span, w)) for i in range(n // span)]
        return jnp.concatenate(parts, axis=0)
    p3 = p.reshape(n // SUBLANES, SUBLANES, w)
    sub = lax.broadcasted_iota(jnp.int32, p3.shape, 1)
    out = None
    for i in range(SUBLANES // span):
        m = i * span + b - 1
        piece = jnp.broadcast_to(p3[:, m:m + 1, :], p3.shape)
        out = piece if out is None else jnp.where(sub >= i * span, piece, out)
    return out.reshape(n, w)


def _hgrn_kernel(x_ref, nw_ref, win_ref, lbl_ref, hnw_ref, wout_ref, o_ref,
                 proj_ref, obf_ref, st_ref, *, tl, layer):
    d_key = lbl_ref.shape[1]
    n_heads = d_key // H_HEAD
    f_off, v_off, g_off = d_key, 2 * d_key, 2 * d_key + n_heads * H_HEAD

    @pl.when(pl.program_id(1) == 0)
    def _():
        st_ref[...] = jnp.zeros_like(st_ref)

    x = x_ref[...]
    u = _rms(x, nw_ref[...]).astype(BF16)
    proj_ref[...] = _dot(u, win_ref[...])

    logits = lbl_ref[...]
    e = jnp.exp(logits - jnp.max(logits, axis=0, keepdims=True))
    lb = jnp.sum(e[1:layer + 1, :], axis=0, keepdims=True) / jnp.sum(e, axis=0, keepdims=True)

    row = lax.broadcasted_iota(jnp.int32, (CHUNK, CHUNK), 0)
    col = lax.broadcasted_iota(jnp.int32, (CHUNK, CHUNK), 1)
    tril_bf = jnp.where(row >= col, 1.0, 0.0).astype(BF16)
    levels = []
    k = 0
    while (1 << k) < CHUNK:
        right = ((row >> k) & 1) == 1
        pair = right & (((col >> k) & 1) == 0) & ((row >> (k + 1)) == (col >> (k + 1)))
        levels.append((k, right, jnp.where(right, 1.0, -1.0), pair))
        k += 1

    def chunk_body(c, carry):
        r0 = pl.multiple_of(c * CHUNK, CHUNK)
        rows = pl.ds(r0, CHUNK)
        forget = lb + (1.0 - lb) * jax.nn.sigmoid(proj_ref[rows, f_off:f_off + d_key])
        p_all = _cumsum_rows(jnp.log2(forget), tril_bf)
        for h in range(n_heads):
            hc = slice(h * H_HEAD, (h + 1) * H_HEAD)
            q = _silu(proj_ref[rows, h * H_HEAD:(h + 1) * H_HEAD])
            fg = forget[:, hc]
            kk = 1.0 - fg
            v_bf = proj_ref[rows, v_off + h * H_HEAD:v_off + (h + 1) * H_HEAD].astype(BF16)
            p = p_all[:, hc]
            s_in = st_ref[h]
            o = _dot((q * jnp.exp2(p)).astype(BF16), s_in.astype(BF16))
            q_bf = q.astype(BF16)
            kk_bf = kk.astype(BF16)
            a = jnp.where(row == col, _dot_nt(q_bf, kk_bf), 0.0)
            for (lg, right, sign, pair) in levels:
                if lg == 0:
                    e_bf = jnp.where(right, fg, 1.0).astype(BF16)
                else:
                    e_bf = jnp.exp2((p - _block_reference(p, 1 << lg)) * sign).astype(BF16)
                a = jnp.where(pair, _dot_nt(q_bf * e_bf, kk_bf * e_bf), a)
            o = o + _dot(a.astype(BF16), v_bf)
            p_last = p[CHUNK - 1:CHUNK, :]
            k_end = kk * jnp.exp2(p_last - p)
            decay_col = jnp.broadcast_to(jnp.exp2(p_last), (CHUNK, H_HEAD)).T
            st_ref[h] = decay_col * s_in + _dot(k_end.T.astype(BF16), v_bf)
            o = o * lax.rsqrt(jnp.mean(o * o, axis=-1, keepdims=True) + EPS) * hnw_ref[...]
            o = o * _silu(proj_ref[rows, g_off + h * H_HEAD:g_off + (h + 1) * H_HEAD])
            obf_ref[rows, hc] = o.astype(BF16)
        return carry

    lax.fori_loop(0, tl // CHUNK, chunk_body, 0)
    o_ref[...] = x + _dot(obf_ref[...], wout_ref[...])


def _hgrn(x, nw, w_in, lb_logits, hnw, w_out, *, layer, tl=512):
    b, l, d = x.shape
    depth, d_key = lb_logits.shape
    d_proj = w_in.shape[1]
    d_val = w_out.shape[0]
    n_heads = d_key // H_HEAD
    return pl.pallas_call(
        functools.partial(_hgrn_kernel, tl=tl, layer=layer),
        out_shape=jax.ShapeDtypeStruct((b, l, d), F32),
        grid=(b, l // tl),
        in_specs=[
            pl.BlockSpec((None, tl, d), lambda i, j: (i, j, 0)),
            _const_spec((1, d)),
            _const_spec((d, d_proj)),
            _const_spec((depth, d_key)),
            _const_spec((1, H_HEAD)),
            _const_spec((d_val, d)),
        ],
        out_specs=pl.BlockSpec((None, tl, d), lambda i, j: (i, j, 0)),
        scratch_shapes=[
            pltpu.VMEM((tl, d_proj), F32),
            pltpu.VMEM((tl, d_val), BF16),
            pltpu.VMEM((n_heads, H_HEAD, H_HEAD), F32),
        ],
        compiler_params=pltpu.CompilerParams(
            dimension_semantics=("arbitrary", "arbitrary"), vmem_limit_bytes=VMEM_LIMIT),
        name="hgrn",
    )(x, nw.reshape(1, d), w_in.astype(BF16), lb_logits, hnw.reshape(1, H_HEAD), w_out.astype(BF16))


def kernel(x, norm_w, ffn_w_gate, ffn_w_up, ffn_w_down, m_w_in, m_conv_w, m_conv_b, m_dt_bias, m_a_log,
           m_d, m_norm_w, m_w_out, h_w_in, h_lb_logits, h_norm_w, h_w_out, final_norm_w):
    b, l, d = x.shape
    depth = norm_w.shape[0]
    fnw = final_norm_w.reshape(1, d)

    def ffn(x, i, j, final=False):
        y = _ffn(x.reshape(b * l, d), norm_w[i, 2 * j].reshape(1, d), ffn_w_gate[i, j].astype(BF16),
                 ffn_w_up[i, j].astype(BF16), ffn_w_down[i, j].astype(BF16), fnw, final=final)
        return y.reshape(b, l, d)

    for i in range(depth):
        x = ffn(x, i, 0)
        j = i // 2
        if i % 2 == 0:
            x = _mamba(x, norm_w[i, 1], m_w_in[j], m_conv_w[j], m_conv_b[j], m_dt_bias[j], m_a_log[j],
                       m_d[j], m_norm_w[j], m_w_out[j])
        else:
            x = _hgrn(x, norm_w[i, 1], h_w_in[j], h_lb_logits, h_norm_w[j], h_w_out[j], layer=i)
        x = ffn(x, i, 1, final=(i == depth - 1))
    return x
```

```python
import functools

import jax
import jax.numpy as jnp
from jax import lax
from jax.experimental import pallas as pl
from jax.experimental.pallas import tpu as pltpu

EPS = 1e-6
LOG2E = 1.4426950408889634
F32 = jnp.float32
BF16 = jnp.bfloat16

M_HEADDIM = 64
M_GROUPS = 4
M_STATE = 128
M_CONV = 4
H_HEAD = 128

CHUNK = 128
SUBLANES = 8
LANES = 128
COL_BLOCK = 256
VMEM_LIMIT = 56 * 1024 * 1024


def _dot(a, b):
    return jnp.dot(a, b, preferred_element_type=F32)


def _dot_nt(a, b):
    return lax.dot_general(a, b, (((1,), (1,)), ((), ())), preferred_element_type=F32)


def _rms(x, w):
    return x * lax.rsqrt(jnp.mean(x * x, axis=-1, keepdims=True) + EPS) * w


def _silu(x):
    return x * jax.nn.sigmoid(x)


def _cumsum_rows(x, tril_bf):
    hi = x.astype(BF16)
    lo = (x - hi.astype(F32)).astype(BF16)
    return _dot(tril_bf, hi) + _dot(tril_bf, lo)


def _const_spec(shape, lead=()):
    block = (None,) * len(lead) + tuple(shape)
    index = tuple(lead) + (0,) * len(shape)
    return pl.BlockSpec(block, lambda *_: index, pipeline_mode=pl.Buffered(1))


def _ffn_kernel(x_ref, nw_ref, wg_ref, wu_ref, wd_ref, fnw_ref, o_ref, h_ref, *, f_chunk, final):
    x = x_ref[...]
    xn = _rms(x, nw_ref[...]).astype(BF16)
    d_ff = wg_ref.shape[1]
    for c in range(d_ff // f_chunk):
        sl = slice(c * f_chunk, (c + 1) * f_chunk)
        g = _dot(xn, wg_ref[:, sl])
        u = _dot(xn, wu_ref[:, sl])
        h_ref[:, sl] = (_silu(g) * u).astype(BF16)
    y = x + 0.5 * _dot(h_ref[...], wd_ref[...])
    if final:
        y = _rms(y, fnw_ref[...])
    o_ref[...] = y


def _ffn(x2d, nw, wg, wu, wd, fnw, *, which, final, tm=512, f_chunk=256):
    t, d = x2d.shape
    d_ff = wg.shape[-1]
    return pl.pallas_call(
        functools.partial(_ffn_kernel, f_chunk=f_chunk, final=final),
        out_shape=jax.ShapeDtypeStruct((t, d), F32),
        grid=(t // tm,),
        in_specs=[
            pl.BlockSpec((tm, d), lambda i: (i, 0)),
            _const_spec((1, d)),
            _const_spec((d, d_ff), which),
            _const_spec((d, d_ff), which),
            _const_spec((d_ff, d), which),
            _const_spec((1, d)),
        ],
        out_specs=pl.BlockSpec((tm, d), lambda i: (i, 0)),
        scratch_shapes=[pltpu.VMEM((tm, d_ff), BF16)],
        compiler_params=pltpu.CompilerParams(
            dimension_semantics=("arbitrary",), vmem_limit_bytes=VMEM_LIMIT),
        name="ffn_final" if final else "ffn",
    )(x2d, nw, wg, wu, wd, fnw)


def _lane_bcast(col):
    return jnp.broadcast_to(col, (col.shape[0], 128))


def _mamba_kernel(x_ref, nw_ref, perm_ref, win_ref, wdt_ref, cw_ref, cb_ref, dtb_ref, alog_ref,
                  dexp_ref, gnw_ref, wout_ref, o_ref,
                  z_ref, xc_ref, dt_ref, tail_ref, ybf_ref, st_ref, *, tl):
    d_inner = z_ref.shape[0] * LANES
    conv_dim = xc_ref.shape[0] * LANES
    n_pairs = d_inner // LANES
    pairs_per_group = n_pairs // M_GROUPS
    gwidth = d_inner // M_GROUPS
    b_blk = d_inner // LANES
    c_blk = b_blk + M_GROUPS * M_STATE // LANES
    seg = tl // SUBLANES

    @pl.when(pl.program_id(1) == 0)
    def _():
        st_ref[...] = jnp.zeros_like(st_ref)
        tail_ref[...] = jnp.zeros_like(tail_ref)

    def store_blocks(ref, first_block, val):
        for q in range(val.shape[1] // LANES):
            ref[first_block + q] = val[:, q * LANES:(q + 1) * LANES]

    def load_chunk(ref, block, c):
        per_chunk = CHUNK // seg
        return jnp.concatenate(
            [ref[block, pl.ds(c * per_chunk + e, seg, stride=SUBLANES), :] for e in range(per_chunk)], axis=0)

    u = _dot(perm_ref[...], _rms(x_ref[...], nw_ref[...]).astype(BF16)).astype(BF16)
    store_blocks(dt_ref, 0, _dot(u, wdt_ref[...]))
    first_sublane = lax.broadcasted_iota(jnp.int32, (SUBLANES, COL_BLOCK), 0) == 0
    n_xbc_blocks = conv_dim // COL_BLOCK
    n_z_blocks = d_inner // COL_BLOCK
    lane_blocks = COL_BLOCK // LANES
    for j in range(max(n_xbc_blocks, n_z_blocks)):
        cols = slice(j * COL_BLOCK, (j + 1) * COL_BLOCK)
        if j < n_xbc_blocks:
            xr = _dot(u, win_ref[:, d_inner + j * COL_BLOCK:d_inner + (j + 1) * COL_BLOCK])
        if j < n_z_blocks:
            store_blocks(z_ref, j * lane_blocks, _dot(u, win_ref[:, cols]))
        if j < n_xbc_blocks:
            history = []
            for k in range(M_CONV - 1, 0, -1):
                group = xr[(seg - k) * SUBLANES:(seg - k + 1) * SUBLANES, :]
                history.append(jnp.where(first_sublane, tail_ref[k - 1:k, cols], pltpu.roll(group, 1, axis=0)))
            for k in range(1, M_CONV):
                tail_ref[k - 1:k, cols] = xr[tl - k * SUBLANES + SUBLANES - 1:tl - k * SUBLANES + SUBLANES, :]
            xext = jnp.concatenate(history + [xr], axis=0)
            acc = cb_ref[:, cols]
            for k in range(M_CONV):
                acc = acc + cw_ref[k:k + 1, cols] * xext[k * SUBLANES:k * SUBLANES + tl, :]
            store_blocks(xc_ref, j * lane_blocks, _silu(acc))

    a_neg2 = -jnp.exp(alog_ref[...]) * LOG2E
    row = lax.broadcasted_iota(jnp.int32, (CHUNK, CHUNK), 0)
    col = lax.broadcasted_iota(jnp.int32, (CHUNK, CHUNK), 1)
    tril = row >= col
    tril_bf = jnp.where(tril, 1.0, 0.0).astype(BF16)
    first_lanes = col < M_HEADDIM
    first_rows = row < M_HEADDIM

    def chunk_body(c, carry):
        r0 = pl.multiple_of(c * CHUNK, CHUNK)
        rows = pl.ds(r0, CHUNK)
        dt_raw = load_chunk(dt_ref, 0, c) + dtb_ref[...]
        dt = jnp.maximum(dt_raw, 0.0) + jnp.log1p(jnp.exp(-jnp.abs(dt_raw)))
        cs = _cumsum_rows(dt * a_neg2, tril_bf)
        cs_t = cs.T
        dt_t = dt.T
        cs_last_t = cs_t[:, CHUNK - 1:CHUNK]
        w_t = jnp.exp2(cs_last_t - cs_t) * dt_t
        cd_t = jnp.exp2(cs_last_t)
        src_t = cs_t - jnp.log2(dt_t)

        for g in range(M_GROUPS):
            bg_bf = load_chunk(xc_ref, b_blk + g, c).astype(BF16)
            cg_bf = load_chunk(xc_ref, c_blk + g, c).astype(BF16)
            cb = _dot_nt(cg_bf, bg_bf)
            grows = slice(g * gwidth, (g + 1) * gwidth)
            y_off = _dot_nt(cg_bf, st_ref[grows, :].astype(BF16))
            y_parts = []
            for jj in range(pairs_per_group):
                j = g * pairs_per_group + jj
                h0 = 2 * j
                pcols = slice(j * 128, (j + 1) * 128)
                xs = load_chunk(xc_ref, j, c)
                xs_bf = xs.astype(BF16)
                yd, ecol = [], []
                for h in (h0, h0 + 1):
                    c_col = _lane_bcast(cs[:, h:h + 1])
                    lmat = jnp.exp2(jnp.where(tril, c_col - src_t[h:h + 1, :], -jnp.inf))
                    m = (cb * lmat).astype(BF16)
                    yd.append(_dot(m, xs_bf))
                    ecol.append(jnp.exp2(c_col))
                y = jnp.where(first_lanes, yd[0], yd[1])
                y = y + jnp.where(first_lanes, ecol[0], ecol[1]) * y_off[:, jj * 128:(jj + 1) * 128]
                y_parts.append(y + dexp_ref[:, pcols] * xs)
                w_rows = jnp.where(first_rows, w_t[h0:h0 + 1, :], w_t[h0 + 1:h0 + 2, :])
                cd_rows = jnp.where(first_rows, cd_t[h0:h0 + 1, :], cd_t[h0 + 1:h0 + 2, :])
                xw_t = (xs.T * w_rows).astype(BF16)
                st_ref[pcols, :] = cd_rows * st_ref[pcols, :] + _dot(xw_t, bg_bf)
            zg = jnp.concatenate([load_chunk(z_ref, g * pairs_per_group + jj, c) for jj in range(pairs_per_group)],
                                 axis=1)
            yg = jnp.concatenate(y_parts, axis=1) * _silu(zg)
            yg = yg * lax.rsqrt(jnp.mean(yg * yg, axis=-1, keepdims=True) + EPS)
            ybf_ref[rows, grows] = (yg * gnw_ref[:, grows]).astype(BF16)
        return carry

    lax.fori_loop(0, tl // CHUNK, chunk_body, 0, unroll=2)
    o_ref[...] = x_ref[...] + _dot(ybf_ref[...], wout_ref[...])


def _mamba(x, nw, w_in, w_out, conv_w, conv_b, dt_bias, a_log, d_skip, gnw, *, which, tl=512):
    b, l, d = x.shape
    n_heads = dt_bias.shape[0]
    d_inner = n_heads * M_HEADDIM
    conv_dim = d_inner + 2 * M_GROUPS * M_STATE
    d_proj = w_in.shape[-1]
    pad = 128 - n_heads
    wdt = jnp.pad(w_in[which, :, d_inner + conv_dim:], ((0, 0), (0, pad)))
    dtb = jnp.pad(dt_bias, (0, pad)).reshape(1, 128)
    alog = jnp.pad(a_log, (0, pad)).reshape(1, 128)
    dexp = jnp.repeat(d_skip, M_HEADDIM).reshape(1, d_inner)
    r = jnp.arange(tl)
    perm = (jnp.arange(tl)[None, :] == ((r % SUBLANES) * (tl // SUBLANES) + r // SUBLANES)[:, None]).astype(BF16)
    return pl.pallas_call(
        functools.partial(_mamba_kernel, tl=tl),
        out_shape=jax.ShapeDtypeStruct((b, l, d), F32),
        grid=(b, l // tl),
        in_specs=[
            pl.BlockSpec((None, tl, d), lambda i, j: (i, j, 0)),
            _const_spec((1, d)),
            _const_spec((tl, tl)),
            _const_spec((d, d_proj), (which,)),
            _const_spec((d, 128)),
            _const_spec((M_CONV, conv_dim)),
            _const_spec((1, conv_dim)),
            _const_spec((1, 128)),
            _const_spec((1, 128)),
            _const_spec((1, d_inner)),
            _const_spec((1, d_inner)),
            _const_spec((d_inner, d), (which,)),
        ],
        out_specs=pl.BlockSpec((None, tl, d), lambda i, j: (i, j, 0)),
        scratch_shapes=[
            pltpu.VMEM((d_inner // LANES, tl, LANES), F32),
            pltpu.VMEM((conv_dim // LANES, tl, LANES), F32),
            pltpu.VMEM((1, tl, LANES), F32),
            pltpu.VMEM((SUBLANES, conv_dim), F32),
            pltpu.VMEM((tl, d_inner), BF16),
            pltpu.VMEM((d_inner, M_STATE), F32),
        ],
        compiler_params=pltpu.CompilerParams(
            dimension_semantics=("arbitrary", "arbitrary"), vmem_limit_bytes=VMEM_LIMIT),
        name="mamba",
    )(x, nw.reshape(1, d), perm, w_in, wdt, conv_w, conv_b.reshape(1, conv_dim), dtb, alog, dexp,
      gnw.reshape(1, d_inner), w_out)


def _block_reference(p, b):
    n, w = p.shape
    span = 2 * b
    if span >= 2 * SUBLANES:
        parts = [jnp.broadcast_to(p[i * span + b - 1:i * span + b, :], (span, w)) for i in range(n // span)]
        return jnp.concatenate(parts, axis=0)
    p3 = p.reshape(n // SUBLANES, SUBLANES, w)
    sub = lax.broadcasted_iota(jnp.int32, p3.shape, 1)
    out = None
    for i in range(SUBLANES // span):
        m = i * span + b - 1
        piece = jnp.broadcast_to(p3[:, m:m + 1, :], p3.shape)
        out = piece if out is None else jnp.where(sub >= i * span, piece, out)
    return out.reshape(n, w)


def _hgrn_kernel(x_ref, nw_ref, win_ref, lbl_ref, hnw_ref, wout_ref, o_ref,
                 proj_ref, obf_ref, st_ref, *, tl, layer):
    d_key = lbl_ref.shape[1]
    n_heads = d_key // H_HEAD
    f_off, v_off, g_off = d_key, 2 * d_key, 2 * d_key + n_heads * H_HEAD

    @pl.when(pl.program_id(1) == 0)
    def _():
        st_ref[...] = jnp.zeros_like(st_ref)

    x = x_ref[...]
    u = _rms(x, nw_ref[...]).astype(BF16)
    proj_ref[...] = _dot(u, win_ref[...])

    logits = lbl_ref[...]
    e = jnp.exp(logits - jnp.max(logits, axis=0, keepdims=True))
    lb = jnp.sum(e[1:layer + 1, :], axis=0, keepdims=True) / jnp.sum(e, axis=0, keepdims=True)

    row = lax.broadcasted_iota(jnp.int32, (CHUNK, CHUNK), 0)
    col = lax.broadcasted_iota(jnp.int32, (CHUNK, CHUNK), 1)
    tril_bf = jnp.where(row >= col, 1.0, 0.0).astype(BF16)
    levels = []
    k = 0
    while (1 << k) < CHUNK:
        right = ((row >> k) & 1) == 1
        pair = right & (((col >> k) & 1) == 0) & ((row >> (k + 1)) == (col >> (k + 1)))
        levels.append((k, right, jnp.where(right, 1.0, -1.0), pair))
        k += 1

    def chunk_body(c, carry):
        r0 = pl.multiple_of(c * CHUNK, CHUNK)
        rows = pl.ds(r0, CHUNK)
        forget = lb + (1.0 - lb) * jax.nn.sigmoid(proj_ref[rows, f_off:f_off + d_key])
        p_all = _cumsum_rows(jnp.log2(forget), tril_bf)
        for h in range(n_heads):
            hc = slice(h * H_HEAD, (h + 1) * H_HEAD)
            q = _silu(proj_ref[rows, h * H_HEAD:(h + 1) * H_HEAD])
            fg = forget[:, hc]
            kk = 1.0 - fg
            v_bf = proj_ref[rows, v_off + h * H_HEAD:v_off + (h + 1) * H_HEAD].astype(BF16)
            p = p_all[:, hc]
            s_in = st_ref[h]
            o = _dot((q * jnp.exp2(p)).astype(BF16), s_in.astype(BF16))
            q_bf = q.astype(BF16)
            kk_bf = kk.astype(BF16)
            a = jnp.where(row == col, _dot_nt(q_bf, kk_bf), 0.0)
            for (lg, right, sign, pair) in levels:
                if lg == 0:
                    e_bf = jnp.where(right, fg, 1.0).astype(BF16)
                else:
                    e_bf = jnp.exp2((p - _block_reference(p, 1 << lg)) * sign).astype(BF16)
                a = jnp.where(pair, _dot_nt(q_bf * e_bf, kk_bf * e_bf), a)
            o = o + _dot(a.astype(BF16), v_bf)
            p_last = p[CHUNK - 1:CHUNK, :]
            k_end = kk * jnp.exp2(p_last - p)
            decay_col = jnp.broadcast_to(jnp.exp2(p_last), (CHUNK, H_HEAD)).T
            st_ref[h] = decay_col * s_in + _dot(k_end.T.astype(BF16), v_bf)
            o = o * lax.rsqrt(jnp.mean(o * o, axis=-1, keepdims=True) + EPS) * hnw_ref[...]
            o = o * _silu(proj_ref[rows, g_off + h * H_HEAD:g_off + (h + 1) * H_HEAD])
            obf_ref[rows, hc] = o.astype(BF16)
        return carry

    lax.fori_loop(0, tl // CHUNK, chunk_body, 0, unroll=2)
    o_ref[...] = x + _dot(obf_ref[...], wout_ref[...])


def _hgrn(x, nw, w_in, w_out, lb_logits, hnw, *, which, layer, tl=512):
    b, l, d = x.shape
    depth, d_key = lb_logits.shape
    d_proj = w_in.shape[-1]
    d_val = w_out.shape[-2]
    n_heads = d_key // H_HEAD
    return pl.pallas_call(
        functools.partial(_hgrn_kernel, tl=tl, layer=layer),
        out_shape=jax.ShapeDtypeStruct((b, l, d), F32),
        grid=(b, l // tl),
        in_specs=[
            pl.BlockSpec((None, tl, d), lambda i, j: (i, j, 0)),
            _const_spec((1, d)),
            _const_spec((d, d_proj), (which,)),
            _const_spec((depth, d_key)),
            _const_spec((1, H_HEAD)),
            _const_spec((d_val, d), (which,)),
        ],
        out_specs=pl.BlockSpec((None, tl, d), lambda i, j: (i, j, 0)),
        scratch_shapes=[
            pltpu.VMEM((tl, d_proj), F32),
            pltpu.VMEM((tl, d_val), BF16),
            pltpu.VMEM((n_heads, H_HEAD, H_HEAD), F32),
        ],
        compiler_params=pltpu.CompilerParams(
            dimension_semantics=("arbitrary", "arbitrary"), vmem_limit_bytes=VMEM_LIMIT),
        name="hgrn",
    )(x, nw.reshape(1, d), w_in, lb_logits, hnw.reshape(1, H_HEAD), w_out)


def kernel(x, norm_w, ffn_w_gate, ffn_w_up, ffn_w_down, m_w_in, m_conv_w, m_conv_b, m_dt_bias, m_a_log,
           m_d, m_norm_w, m_w_out, h_w_in, h_lb_logits, h_norm_w, h_w_out, final_norm_w):
    b, l, d = x.shape
    depth = norm_w.shape[0]
    fnw = final_norm_w.reshape(1, d)
    wg, wu, wd = ffn_w_gate.astype(BF16), ffn_w_up.astype(BF16), ffn_w_down.astype(BF16)
    m_in, m_out = m_w_in.astype(BF16), m_w_out.astype(BF16)
    h_in, h_out = h_w_in.astype(BF16), h_w_out.astype(BF16)

    def ffn(x, i, j, final=False):
        y = _ffn(x.reshape(b * l, d), norm_w[i, 2 * j].reshape(1, d), wg, wu, wd, fnw,
                 which=(i, j), final=final)
        return y.reshape(b, l, d)

    for i in range(depth):
        x = ffn(x, i, 0)
        j = i // 2
        if i % 2 == 0:
            x = _mamba(x, norm_w[i, 1], m_in, m_out, m_conv_w[j], m_conv_b[j], m_dt_bias[j], m_a_log[j],
                       m_d[j], m_norm_w[j], which=j)
        else:
            x = _hgrn(x, norm_w[i, 1], h_in, h_out, h_lb_logits, h_norm_w[j], which=j, layer=i)
        x = ffn(x, i, 1, final=(i == depth - 1))
    return x
```

```python
import functools

import jax
import jax.numpy as jnp
from jax import lax
from jax.experimental import pallas as pl
from jax.experimental.pallas import tpu as pltpu

EPS = 1e-6
LOG2E = 1.4426950408889634
F32 = jnp.float32
BF16 = jnp.bfloat16

M_HEADDIM = 64
M_GROUPS = 4
M_STATE = 128
M_CONV = 4
H_HEAD = 128

CHUNK = 128
SUBLANES = 8
LANES = 128
COL_BLOCK = 256
VMEM_LIMIT = 56 * 1024 * 1024


def _dot(a, b):
    return jnp.dot(a, b, preferred_element_type=F32)


def _dot_nt(a, b):
    return lax.dot_general(a, b, (((1,), (1,)), ((), ())), preferred_element_type=F32)


def _rms(x, w):
    return x * lax.rsqrt(jnp.mean(x * x, axis=-1, keepdims=True) + EPS) * w


def _silu(x):
    return x * jax.nn.sigmoid(x)


def _cumsum_rows(x, tril_bf):
    hi = x.astype(BF16)
    lo = (x - hi.astype(F32)).astype(BF16)
    return _dot(tril_bf, hi) + _dot(tril_bf, lo)


def _const_spec(shape, lead=()):
    block = (None,) * len(lead) + tuple(shape)
    index = tuple(lead) + (0,) * len(shape)
    return pl.BlockSpec(block, lambda *_: index, pipeline_mode=pl.Buffered(1))


def _ffn_kernel(x_ref, nw_ref, wg_ref, wu_ref, wd_ref, fnw_ref, o_ref, h_ref, *, f_chunk, final):
    x = x_ref[...]
    xn = _rms(x, nw_ref[...]).astype(BF16)
    d_ff = wg_ref.shape[1]
    for c in range(d_ff // f_chunk):
        sl = slice(c * f_chunk, (c + 1) * f_chunk)
        g = _dot(xn, wg_ref[:, sl])
        u = _dot(xn, wu_ref[:, sl])
        h_ref[:, sl] = (_silu(g) * u).astype(BF16)
    y = x + 0.5 * _dot(h_ref[...], wd_ref[...])
    if final:
        y = _rms(y, fnw_ref[...])
    o_ref[...] = y


def _ffn(x2d, nw, wg, wu, wd, fnw, *, which, final, tm=1024, f_chunk=256):
    t, d = x2d.shape
    d_ff = wg.shape[-1]
    return pl.pallas_call(
        functools.partial(_ffn_kernel, f_chunk=f_chunk, final=final),
        out_shape=jax.ShapeDtypeStruct((t, d), F32),
        grid=(t // tm,),
        in_specs=[
            pl.BlockSpec((tm, d), lambda i: (i, 0)),
            _const_spec((1, d)),
            _const_spec((d, d_ff), which),
            _const_spec((d, d_ff), which),
            _const_spec((d_ff, d), which),
            _const_spec((1, d)),
        ],
        out_specs=pl.BlockSpec((tm, d), lambda i: (i, 0)),
        scratch_shapes=[pltpu.VMEM((tm, d_ff), BF16)],
        compiler_params=pltpu.CompilerParams(
            dimension_semantics=("arbitrary",), vmem_limit_bytes=VMEM_LIMIT),
        name="ffn_final" if final else "ffn",
    )(x2d, nw, wg, wu, wd, fnw)


def _lane_bcast(col):
    return jnp.broadcast_to(col, (col.shape[0], 128))


def _mamba_kernel(x_ref, nw_ref, perm_ref, win_ref, wdt_ref, cw_ref, cb_ref, dtb_ref, alog_ref,
                  dexp_ref, gnw_ref, wout_ref, o_ref,
                  z_ref, xc_ref, dt_ref, tail_ref, ybf_ref, st_ref, *, tl):
    d_inner = z_ref.shape[0] * LANES
    conv_dim = xc_ref.shape[0] * LANES
    n_pairs = d_inner // LANES
    pairs_per_group = n_pairs // M_GROUPS
    gwidth = d_inner // M_GROUPS
    b_blk = d_inner // LANES
    c_blk = b_blk + M_GROUPS * M_STATE // LANES
    seg = tl // SUBLANES

    @pl.when(pl.program_id(1) == 0)
    def _():
        st_ref[...] = jnp.zeros_like(st_ref)
        tail_ref[...] = jnp.zeros_like(tail_ref)

    def store_blocks(ref, first_block, val):
        for q in range(val.shape[1] // LANES):
            ref[first_block + q] = val[:, q * LANES:(q + 1) * LANES]

    def load_chunk(ref, block, c):
        per_chunk = CHUNK // seg
        return jnp.concatenate(
            [ref[block, pl.ds(c * per_chunk + e, seg, stride=SUBLANES), :] for e in range(per_chunk)], axis=0)

    u = _dot(perm_ref[...], _rms(x_ref[...], nw_ref[...]).astype(BF16)).astype(BF16)
    store_blocks(dt_ref, 0, _dot(u, wdt_ref[...]))
    first_sublane = lax.broadcasted_iota(jnp.int32, (SUBLANES, COL_BLOCK), 0) == 0
    n_xbc_blocks = conv_dim // COL_BLOCK
    n_z_blocks = d_inner // COL_BLOCK
    lane_blocks = COL_BLOCK // LANES
    for j in range(max(n_xbc_blocks, n_z_blocks)):
        cols = slice(j * COL_BLOCK, (j + 1) * COL_BLOCK)
        if j < n_xbc_blocks:
            xr = _dot(u, win_ref[:, d_inner + j * COL_BLOCK:d_inner + (j + 1) * COL_BLOCK])
        if j < n_z_blocks:
            store_blocks(z_ref, j * lane_blocks, _dot(u, win_ref[:, cols]))
        if j < n_xbc_blocks:
            history = []
            for k in range(M_CONV - 1, 0, -1):
                group = xr[(seg - k) * SUBLANES:(seg - k + 1) * SUBLANES, :]
                history.append(jnp.where(first_sublane, tail_ref[k - 1:k, cols], pltpu.roll(group, 1, axis=0)))
            for k in range(1, M_CONV):
                tail_ref[k - 1:k, cols] = xr[tl - k * SUBLANES + SUBLANES - 1:tl - k * SUBLANES + SUBLANES, :]
            xext = jnp.concatenate(history + [xr], axis=0)
            acc = cb_ref[:, cols]
            for k in range(M_CONV):
                acc = acc + cw_ref[k:k + 1, cols] * xext[k * SUBLANES:k * SUBLANES + tl, :]
            store_blocks(xc_ref, j * lane_blocks, _silu(acc))

    a_neg2 = -jnp.exp(alog_ref[...]) * LOG2E
    row = lax.broadcasted_iota(jnp.int32, (CHUNK, CHUNK), 0)
    col = lax.broadcasted_iota(jnp.int32, (CHUNK, CHUNK), 1)
    tril = row >= col
    tril_bf = jnp.where(tril, 1.0, 0.0).astype(BF16)
    first_lanes = col < M_HEADDIM
    first_rows = row < M_HEADDIM

    def chunk_body(c, carry):
        r0 = pl.multiple_of(c * CHUNK, CHUNK)
        rows = pl.ds(r0, CHUNK)
        dt_raw = load_chunk(dt_ref, 0, c) + dtb_ref[...]
        dt = jnp.maximum(dt_raw, 0.0) + jnp.log1p(jnp.exp(-jnp.abs(dt_raw)))
        cs = _cumsum_rows(dt * a_neg2, tril_bf)
        cs_t = cs.T
        dt_t = dt.T
        cs_last_t = cs_t[:, CHUNK - 1:CHUNK]
        w_t = jnp.exp2(cs_last_t - cs_t) * dt_t
        cd_t = jnp.exp2(cs_last_t)
        src_t = cs_t - jnp.log2(dt_t)

        for g in range(M_GROUPS):
            bg_bf = load_chunk(xc_ref, b_blk + g, c).astype(BF16)
            cg_bf = load_chunk(xc_ref, c_blk + g, c).astype(BF16)
            cb = _dot_nt(cg_bf, bg_bf)
            grows = slice(g * gwidth, (g + 1) * gwidth)
            y_off = _dot_nt(cg_bf, st_ref[grows, :].astype(BF16))
            y_parts = []
            for jj in range(pairs_per_group):
                j = g * pairs_per_group + jj
                h0 = 2 * j
                pcols = slice(j * 128, (j + 1) * 128)
                xs = load_chunk(xc_ref, j, c)
                xs_bf = xs.astype(BF16)
                yd, ecol = [], []
                for h in (h0, h0 + 1):
                    c_col = _lane_bcast(cs[:, h:h + 1])
                    lmat = jnp.exp2(jnp.where(tril, c_col - src_t[h:h + 1, :], -jnp.inf))
                    m = (cb * lmat).astype(BF16)
                    yd.append(_dot(m, xs_bf))
                    ecol.append(jnp.exp2(c_col))
                y = jnp.where(first_lanes, yd[0], yd[1])
                y = y + jnp.where(first_lanes, ecol[0], ecol[1]) * y_off[:, jj * 128:(jj + 1) * 128]
                y_parts.append(y + dexp_ref[:, pcols] * xs)
                w_rows = jnp.where(first_rows, w_t[h0:h0 + 1, :], w_t[h0 + 1:h0 + 2, :])
                cd_rows = jnp.where(first_rows, cd_t[h0:h0 + 1, :], cd_t[h0 + 1:h0 + 2, :])
                xw_t = (xs.T * w_rows).astype(BF16)
                st_ref[pcols, :] = cd_rows * st_ref[pcols, :] + _dot(xw_t, bg_bf)
            zg = jnp.concatenate([load_chunk(z_ref, g * pairs_per_group + jj, c) for jj in range(pairs_per_group)],
                                 axis=1)
            yg = jnp.concatenate(y_parts, axis=1) * _silu(zg)
            yg = yg * lax.rsqrt(jnp.mean(yg * yg, axis=-1, keepdims=True) + EPS)
            ybf_ref[rows, grows] = (yg * gnw_ref[:, grows]).astype(BF16)
        return carry

    lax.fori_loop(0, tl // CHUNK, chunk_body, 0, unroll=True)
    o_ref[...] = x_ref[...] + _dot(ybf_ref[...], wout_ref[...])


def _mamba(x, nw, w_in, w_out, conv_w, conv_b, dt_bias, a_log, d_skip, gnw, *, which, tl=512):
    b, l, d = x.shape
    n_heads = dt_bias.shape[0]
    d_inner = n_heads * M_HEADDIM
    conv_dim = d_inner + 2 * M_GROUPS * M_STATE
    d_proj = w_in.shape[-1]
    pad = 128 - n_heads
    wdt = jnp.pad(w_in[which, :, d_inner + conv_dim:], ((0, 0), (0, pad)))
    dtb = jnp.pad(dt_bias, (0, pad)).reshape(1, 128)
    alog = jnp.pad(a_log, (0, pad)).reshape(1, 128)
    dexp = jnp.repeat(d_skip, M_HEADDIM).reshape(1, d_inner)
    r = jnp.arange(tl)
    perm = (jnp.arange(tl)[None, :] == ((r % SUBLANES) * (tl // SUBLANES) + r // SUBLANES)[:, None]).astype(BF16)
    return pl.pallas_call(
        functools.partial(_mamba_kernel, tl=tl),
        out_shape=jax.ShapeDtypeStruct((b, l, d), F32),
        grid=(b, l // tl),
        in_specs=[
            pl.BlockSpec((None, tl, d), lambda i, j: (i, j, 0)),
            _const_spec((1, d)),
            _const_spec((tl, tl)),
            _const_spec((d, d_proj), (which,)),
            _const_spec((d, 128)),
            _const_spec((M_CONV, conv_dim)),
            _const_spec((1, conv_dim)),
            _const_spec((1, 128)),
            _const_spec((1, 128)),
            _const_spec((1, d_inner)),
            _const_spec((1, d_inner)),
            _const_spec((d_inner, d), (which,)),
        ],
        out_specs=pl.BlockSpec((None, tl, d), lambda i, j: (i, j, 0)),
        scratch_shapes=[
            pltpu.VMEM((d_inner // LANES, tl, LANES), F32),
            pltpu.VMEM((conv_dim // LANES, tl, LANES), F32),
            pltpu.VMEM((1, tl, LANES), F32),
            pltpu.VMEM((SUBLANES, conv_dim), F32),
            pltpu.VMEM((tl, d_inner), BF16),
            pltpu.VMEM((d_inner, M_STATE), F32),
        ],
        compiler_params=pltpu.CompilerParams(
            dimension_semantics=("arbitrary", "arbitrary"), vmem_limit_bytes=VMEM_LIMIT),
        name="mamba",
    )(x, nw.reshape(1, d), perm, w_in, wdt, conv_w, conv_b.reshape(1, conv_dim), dtb, alog, dexp,
      gnw.reshape(1, d_inner), w_out)


def _boundary_distance(p, b, sign):
    n, w = p.shape
    span = 2 * b
    if b >= SUBLANES:
        parts = []
        for i in range(n // span):
            ref = p[i * span + b - 1:i * span + b, :]
            parts += [ref - p[i * span:i * span + b, :], p[i * span + b:(i + 1) * span, :] - ref]
        return jnp.concatenate(parts, axis=0)
    p3 = p.reshape(n // SUBLANES, SUBLANES, w)
    sub = lax.broadcasted_iota(jnp.int32, p3.shape, 1)
    ref = None
    for i in range(SUBLANES // span):
        m = i * span + b - 1
        piece = jnp.broadcast_to(p3[:, m:m + 1, :], p3.shape)
        ref = piece if ref is None else jnp.where(sub >= i * span, piece, ref)
    return (p - ref.reshape(n, w)) * sign


def _hgrn_kernel(x_ref, nw_ref, win_ref, lbl_ref, hnw_ref, wout_ref, o_ref,
                 proj_ref, obf_ref, st_ref, *, tl, layer):
    d_key = lbl_ref.shape[1]
    n_heads = d_key // H_HEAD
    f_off, v_off, g_off = d_key, 2 * d_key, 2 * d_key + n_heads * H_HEAD

    @pl.when(pl.program_id(1) == 0)
    def _():
        st_ref[...] = jnp.zeros_like(st_ref)

    x = x_ref[...]
    u = _rms(x, nw_ref[...]).astype(BF16)
    proj_ref[...] = _dot(u, win_ref[...])

    logits = lbl_ref[...]
    e = jnp.exp(logits - jnp.max(logits, axis=0, keepdims=True))
    lb = jnp.sum(e[1:layer + 1, :], axis=0, keepdims=True) / jnp.sum(e, axis=0, keepdims=True)

    row = lax.broadcasted_iota(jnp.int32, (CHUNK, CHUNK), 0)
    col = lax.broadcasted_iota(jnp.int32, (CHUNK, CHUNK), 1)
    tril_bf = jnp.where(row >= col, 1.0, 0.0).astype(BF16)
    levels = []
    k = 0
    while (1 << k) < CHUNK:
        right = ((row >> k) & 1) == 1
        pair = right & (((col >> k) & 1) == 0) & ((row >> (k + 1)) == (col >> (k + 1)))
        levels.append((k, right, jnp.where(right, 1.0, -1.0), pair))
        k += 1

    def chunk_body(c, carry):
        r0 = pl.multiple_of(c * CHUNK, CHUNK)
        rows = pl.ds(r0, CHUNK)
        forget = lb + (1.0 - lb) * jax.nn.sigmoid(proj_ref[rows, f_off:f_off + d_key])
        p_all = _cumsum_rows(jnp.log2(forget), tril_bf)
        for h in range(n_heads):
            hc = slice(h * H_HEAD, (h + 1) * H_HEAD)
            q = _silu(proj_ref[rows, h * H_HEAD:(h + 1) * H_HEAD])
            fg = forget[:, hc]
            kk = 1.0 - fg
            v_bf = proj_ref[rows, v_off + h * H_HEAD:v_off + (h + 1) * H_HEAD].astype(BF16)
            p = p_all[:, hc]
            s_in = st_ref[h]
            q_bf = q.astype(BF16)
            kk_bf = kk.astype(BF16)
            a = jnp.where(row == col, _dot_nt(q_bf, kk_bf), 0.0)
            for (lg, right, sign, pair) in levels:
                if lg == 0:
                    e_bf = jnp.where(right, fg, 1.0).astype(BF16)
                else:
                    e_bf = jnp.exp2(_boundary_distance(p, 1 << lg, sign)).astype(BF16)
                a = jnp.where(pair, _dot_nt(q_bf * e_bf, kk_bf * e_bf), a)
            o = _dot(jnp.concatenate([a.astype(BF16), (q * jnp.exp2(p)).astype(BF16)], axis=1),
                     jnp.concatenate([v_bf, s_in.astype(BF16)], axis=0))
            p_last = p[CHUNK - 1:CHUNK, :]
            k_end = kk * jnp.exp2(p_last - p)
            decay_col = jnp.broadcast_to(jnp.exp2(p_last), (CHUNK, H_HEAD)).T
            st_ref[h] = decay_col * s_in + _dot(k_end.T.astype(BF16), v_bf)
            o = o * lax.rsqrt(jnp.mean(o * o, axis=-1, keepdims=True) + EPS) * hnw_ref[...]
            o = o * _silu(proj_ref[rows, g_off + h * H_HEAD:g_off + (h + 1) * H_HEAD])
            obf_ref[rows, hc] = o.astype(BF16)
        return carry

    lax.fori_loop(0, tl // CHUNK, chunk_body, 0, unroll=True)
    o_ref[...] = x + _dot(obf_ref[...], wout_ref[...])


def _hgrn(x, nw, w_in, w_out, lb_logits, hnw, *, which, layer, tl=512):
    b, l, d = x.shape
    depth, d_key = lb_logits.shape
    d_proj = w_in.shape[-1]
    d_val = w_out.shape[-2]
    n_heads = d_key // H_HEAD
    return pl.pallas_call(
        functools.partial(_hgrn_kernel, tl=tl, layer=layer),
        out_shape=jax.ShapeDtypeStruct((b, l, d), F32),
        grid=(b, l // tl),
        in_specs=[
            pl.BlockSpec((None, tl, d), lambda i, j: (i, j, 0)),
            _const_spec((1, d)),
            _const_spec((d, d_proj), (which,)),
            _const_spec((depth, d_key)),
            _const_spec((1, H_HEAD)),
            _const_spec((d_val, d), (which,)),
        ],
        out_specs=pl.BlockSpec((None, tl, d), lambda i, j: (i, j, 0)),
        scratch_shapes=[
            pltpu.VMEM((tl, d_proj), F32),
            pltpu.VMEM((tl, d_val), BF16),
            pltpu.VMEM((n_heads, H_HEAD, H_HEAD), F32),
        ],
        compiler_params=pltpu.CompilerParams(
            dimension_semantics=("arbitrary", "arbitrary"), vmem_limit_bytes=VMEM_LIMIT),
        name="hgrn",
    )(x, nw.reshape(1, d), w_in, lb_logits, hnw.reshape(1, H_HEAD), w_out)


def kernel(x, norm_w, ffn_w_gate, ffn_w_up, ffn_w_down, m_w_in, m_conv_w, m_conv_b, m_dt_bias, m_a_log,
           m_d, m_norm_w, m_w_out, h_w_in, h_lb_logits, h_norm_w, h_w_out, final_norm_w):
    b, l, d = x.shape
    depth = norm_w.shape[0]
    fnw = final_norm_w.reshape(1, d)
    wg, wu, wd = ffn_w_gate.astype(BF16), ffn_w_up.astype(BF16), ffn_w_down.astype(BF16)
    m_in, m_out = m_w_in.astype(BF16), m_w_out.astype(BF16)
    h_in, h_out = h_w_in.astype(BF16), h_w_out.astype(BF16)

    def ffn(x, i, j, final=False):
        y = _ffn(x.reshape(b * l, d), norm_w[i, 2 * j].reshape(1, d), wg, wu, wd, fnw,
                 which=(i, j), final=final)
        return y.reshape(b, l, d)

    for i in range(depth):
        x = ffn(x, i, 0)
        j = i // 2
        if i % 2 == 0:
            x = _mamba(x, norm_w[i, 1], m_in, m_out, m_conv_w[j], m_conv_b[j], m_dt_bias[j], m_a_log[j],
                       m_d[j], m_norm_w[j], which=j)
        else:
            x = _hgrn(x, norm_w[i, 1], h_in, h_out, h_lb_logits, h_norm_w[j], which=j, layer=i)
        x = ffn(x, i, 1, final=(i == depth - 1))
    return x
```

```python
import functools

import jax
import jax.numpy as jnp
from jax import lax
from jax.experimental import pallas as pl
from jax.experimental.pallas import tpu as pltpu

EPS = 1e-6
LOG2E = 1.4426950408889634
F32 = jnp.float32
BF16 = jnp.bfloat16

M_HEADDIM = 64
M_GROUPS = 4
M_STATE = 128
M_CONV = 4
H_HEAD = 128

CHUNK = 128
SUBLANES = 8
LANES = 128
COL_BLOCK = 256
VMEM_LIMIT = 56 * 1024 * 1024


def _dot(a, b):
    return jnp.dot(a, b, preferred_element_type=F32)


def _dot_nt(a, b):
    return lax.dot_general(a, b, (((1,), (1,)), ((), ())), preferred_element_type=F32)


def _rms(x, w):
    return x * lax.rsqrt(jnp.mean(x * x, axis=-1, keepdims=True) + EPS) * w


def _silu(x):
    return x * jax.nn.sigmoid(x)


def _cumsum_rows(x, tril_bf):
    hi = x.astype(BF16)
    lo = (x - hi.astype(F32)).astype(BF16)
    return _dot(tril_bf, hi) + _dot(tril_bf, lo)


def _const_spec(shape):
    index = (0,) * len(shape)
    return pl.BlockSpec(tuple(shape), lambda *_: index, pipeline_mode=pl.Buffered(1))


BF16_ROWS = 16


def _cast_plumbing(jobs, steps, step_of):
    in_specs, out_specs, out_shapes = [], [], []
    for arr, lead in jobs:
        rows, width = arr.shape[-2:]
        per_step = next(r for r in range(BF16_ROWS, rows + 1, BF16_ROWS) if rows % r == 0 and rows // r <= steps)
        last = rows // per_step - 1
        in_specs.append(pl.BlockSpec(
            (None,) * len(lead) + (per_step, width),
            lambda *g, lead=lead, last=last: tuple(lead) + (jnp.minimum(step_of(*g), last), 0)))
        out_specs.append(pl.BlockSpec((per_step, width), lambda *g, last=last: (jnp.minimum(step_of(*g), last), 0)))
        out_shapes.append(jax.ShapeDtypeStruct((rows, width), BF16))
    return in_specs, out_specs, out_shapes


def _with_casts(body, n_in, n_jobs):
    def kernel_fn(*refs):
        inputs, cast_in = refs[:n_in], refs[n_in:n_in + n_jobs]
        output, cast_out = refs[n_in + n_jobs], refs[n_in + n_jobs + 1:n_in + 2 * n_jobs + 1]
        scratch = refs[n_in + 2 * n_jobs + 1:]
        for src, dst in zip(cast_in, cast_out):
            dst[...] = src[...].astype(BF16)
        body(*inputs, output, *scratch)
    return kernel_fn


def _ffn_kernel(x_ref, nw_ref, wg_ref, wu_ref, wd_ref, fnw_ref, o_ref, h_ref, *, f_chunk, final):
    x = x_ref[...]
    xn = _rms(x, nw_ref[...]).astype(BF16)
    d_ff = wg_ref.shape[1]
    for c in range(d_ff // f_chunk):
        sl = slice(c * f_chunk, (c + 1) * f_chunk)
        g = _dot(xn, wg_ref[:, sl])
        u = _dot(xn, wu_ref[:, sl])
        h_ref[:, sl] = (_silu(g) * u).astype(BF16)
    y = x + 0.5 * _dot(h_ref[...], wd_ref[...])
    if final:
        y = _rms(y, fnw_ref[...])
    o_ref[...] = y


def _ffn(x2d, nw, wg, wu, wd, fnw, cast_jobs, *, final, tm=1024, f_chunk=256):
    t, d = x2d.shape
    d_ff = wg.shape[-1]
    steps = t // tm
    cast_in, cast_out, cast_shapes = _cast_plumbing(cast_jobs, steps, lambda i: i)
    in_specs = [
        pl.BlockSpec((tm, d), lambda i: (i, 0)),
        _const_spec((1, d)),
        _const_spec((d, d_ff)),
        _const_spec((d, d_ff)),
        _const_spec((d_ff, d)),
        _const_spec((1, d)),
    ]
    return pl.pallas_call(
        _with_casts(functools.partial(_ffn_kernel, f_chunk=f_chunk, final=final), len(in_specs), len(cast_jobs)),
        out_shape=[jax.ShapeDtypeStruct((t, d), F32)] + cast_shapes,
        grid=(steps,),
        in_specs=in_specs + cast_in,
        out_specs=[pl.BlockSpec((tm, d), lambda i: (i, 0))] + cast_out,
        scratch_shapes=[pltpu.VMEM((tm, d_ff), BF16)],
        compiler_params=pltpu.CompilerParams(
            dimension_semantics=("arbitrary",), vmem_limit_bytes=VMEM_LIMIT),
        name="ffn_final" if final else "ffn",
    )(x2d, nw, wg, wu, wd, fnw, *[arr for arr, _ in cast_jobs])


def _lane_bcast(col):
    return jnp.broadcast_to(col, (col.shape[0], 128))


def _mamba_kernel(x_ref, nw_ref, perm_ref, win_ref, wdt_ref, cw_ref, cb_ref, dtb_ref, alog_ref,
                  dexp_ref, gnw_ref, wout_ref, o_ref,
                  z_ref, xc_ref, dt_ref, tail_ref, ybf_ref, st_ref, *, tl):
    d_inner = z_ref.shape[0] * LANES
    conv_dim = xc_ref.shape[0] * LANES
    n_pairs = d_inner // LANES
    pairs_per_group = n_pairs // M_GROUPS
    gwidth = d_inner // M_GROUPS
    b_blk = d_inner // LANES
    c_blk = b_blk + M_GROUPS * M_STATE // LANES
    seg = tl // SUBLANES

    @pl.when(pl.program_id(1) == 0)
    def _():
        st_ref[...] = jnp.zeros_like(st_ref)
        tail_ref[...] = jnp.zeros_like(tail_ref)

    def store_blocks(ref, first_block, val):
        for q in range(val.shape[1] // LANES):
            ref[first_block + q] = val[:, q * LANES:(q + 1) * LANES]

    def load_chunk(ref, block, c):
        per_chunk = CHUNK // seg
        return jnp.concatenate(
            [ref[block, pl.ds(c * per_chunk + e, seg, stride=SUBLANES), :] for e in range(per_chunk)], axis=0)

    u = _dot(perm_ref[...], _rms(x_ref[...], nw_ref[...]).astype(BF16)).astype(BF16)
    store_blocks(dt_ref, 0, _dot(u, wdt_ref[...]))
    first_sublane = lax.broadcasted_iota(jnp.int32, (SUBLANES, COL_BLOCK), 0) == 0
    n_xbc_blocks = conv_dim // COL_BLOCK
    n_z_blocks = d_inner // COL_BLOCK
    lane_blocks = COL_BLOCK // LANES
    for j in range(max(n_xbc_blocks, n_z_blocks)):
        cols = slice(j * COL_BLOCK, (j + 1) * COL_BLOCK)
        if j < n_xbc_blocks:
            xr = _dot(u, win_ref[:, d_inner + j * COL_BLOCK:d_inner + (j + 1) * COL_BLOCK])
        if j < n_z_blocks:
            store_blocks(z_ref, j * lane_blocks, _dot(u, win_ref[:, cols]))
        if j < n_xbc_blocks:
            history = []
            for k in range(M_CONV - 1, 0, -1):
                group = xr[(seg - k) * SUBLANES:(seg - k + 1) * SUBLANES, :]
                history.append(jnp.where(first_sublane, tail_ref[k - 1:k, cols], pltpu.roll(group, 1, axis=0)))
            for k in range(1, M_CONV):
                tail_ref[k - 1:k, cols] = xr[tl - k * SUBLANES + SUBLANES - 1:tl - k * SUBLANES + SUBLANES, :]
            xext = jnp.concatenate(history + [xr], axis=0)
            acc = cb_ref[:, cols]
            for k in range(M_CONV):
                acc = acc + cw_ref[k:k + 1, cols] * xext[k * SUBLANES:k * SUBLANES + tl, :]
            store_blocks(xc_ref, j * lane_blocks, _silu(acc))

    a_neg2 = -jnp.exp(alog_ref[...]) * LOG2E
    row = lax.broadcasted_iota(jnp.int32, (CHUNK, CHUNK), 0)
    col = lax.broadcasted_iota(jnp.int32, (CHUNK, CHUNK), 1)
    tril = row >= col
    tril_bf = jnp.where(tril, 1.0, 0.0).astype(BF16)
    first_lanes = col < M_HEADDIM
    first_rows = row < M_HEADDIM

    def chunk_body(c, carry):
        r0 = pl.multiple_of(c * CHUNK, CHUNK)
        rows = pl.ds(r0, CHUNK)
        dt_raw = load_chunk(dt_ref, 0, c) + dtb_ref[...]
        dt = jnp.maximum(dt_raw, 0.0) + jnp.log1p(jnp.exp(-jnp.abs(dt_raw)))
        cs = _cumsum_rows(dt * a_neg2, tril_bf)
        cs_t = cs.T
        dt_t = dt.T
        cs_last_t = cs_t[:, CHUNK - 1:CHUNK]
        w_t = jnp.exp2(cs_last_t - cs_t) * dt_t
        cd_t = jnp.exp2(cs_last_t)
        src_t = cs_t - jnp.log2(dt_t)

        for g in range(M_GROUPS):
            bg_bf = load_chunk(xc_ref, b_blk + g, c).astype(BF16)
            cg_bf = load_chunk(xc_ref, c_blk + g, c).astype(BF16)
            cb = _dot_nt(cg_bf, bg_bf)
            grows = slice(g * gwidth, (g + 1) * gwidth)
            y_off = _dot_nt(cg_bf, st_ref[grows, :].astype(BF16))
            y_parts = []
            for jj in range(pairs_per_group):
                j = g * pairs_per_group + jj
                h0 = 2 * j
                pcols = slice(j * 128, (j + 1) * 128)
                xs = load_chunk(xc_ref, j, c)
                xs_bf = xs.astype(BF16)
                yd, ecol = [], []
                for h in (h0, h0 + 1):
                    c_col = _lane_bcast(cs[:, h:h + 1])
                    lmat = jnp.exp2(jnp.where(tril, c_col - src_t[h:h + 1, :], -jnp.inf))
                    m = (cb * lmat).astype(BF16)
                    yd.append(_dot(m, xs_bf))
                    ecol.append(jnp.exp2(c_col))
                y = jnp.where(first_lanes, yd[0], yd[1])
                y = y + jnp.where(first_lanes, ecol[0], ecol[1]) * y_off[:, jj * 128:(jj + 1) * 128]
                y_parts.append(y + dexp_ref[:, pcols] * xs)
                w_rows = jnp.where(first_rows, w_t[h0:h0 + 1, :], w_t[h0 + 1:h0 + 2, :])
                cd_rows = jnp.where(first_rows, cd_t[h0:h0 + 1, :], cd_t[h0 + 1:h0 + 2, :])
                xw_t = (xs.T * w_rows).astype(BF16)
                st_ref[pcols, :] = cd_rows * st_ref[pcols, :] + _dot(xw_t, bg_bf)
            zg = jnp.concatenate([load_chunk(z_ref, g * pairs_per_group + jj, c) for jj in range(pairs_per_group)],
                                 axis=1)
            yg = jnp.concatenate(y_parts, axis=1) * _silu(zg)
            yg = yg * lax.rsqrt(jnp.mean(yg * yg, axis=-1, keepdims=True) + EPS)
            ybf_ref[rows, grows] = (yg * gnw_ref[:, grows]).astype(BF16)
        return carry

    lax.fori_loop(0, tl // CHUNK, chunk_body, 0, unroll=True)
    o_ref[...] = x_ref[...] + _dot(ybf_ref[...], wout_ref[...])


def _mamba(x, nw, w_in, w_dt, w_out, conv_w, conv_b, dt_bias, a_log, d_skip, gnw, cast_jobs, *, tl=512):
    b, l, d = x.shape
    n_heads = dt_bias.shape[0]
    d_inner = n_heads * M_HEADDIM
    conv_dim = d_inner + 2 * M_GROUPS * M_STATE
    d_proj = w_in.shape[-1]
    pad = 128 - n_heads
    wdt = jnp.pad(w_dt, ((0, 0), (0, pad)))
    dtb = jnp.pad(dt_bias, (0, pad)).reshape(1, 128)
    alog = jnp.pad(a_log, (0, pad)).reshape(1, 128)
    dexp = jnp.repeat(d_skip, M_HEADDIM).reshape(1, d_inner)
    r = jnp.arange(tl)
    perm = (jnp.arange(tl)[None, :] == ((r % SUBLANES) * (tl // SUBLANES) + r // SUBLANES)[:, None]).astype(BF16)
    tiles = l // tl
    cast_in, cast_out, cast_shapes = _cast_plumbing(cast_jobs, b * tiles, lambda i, j: i * tiles + j)
    in_specs = [
        pl.BlockSpec((None, tl, d), lambda i, j: (i, j, 0)),
        _const_spec((1, d)),
        _const_spec((tl, tl)),
        _const_spec((d, d_proj)),
        _const_spec((d, 128)),
        _const_spec((M_CONV, conv_dim)),
        _const_spec((1, conv_dim)),
        _const_spec((1, 128)),
        _const_spec((1, 128)),
        _const_spec((1, d_inner)),
        _const_spec((1, d_inner)),
        _const_spec((d_inner, d)),
    ]
    return pl.pallas_call(
        _with_casts(functools.partial(_mamba_kernel, tl=tl), len(in_specs), len(cast_jobs)),
        out_shape=[jax.ShapeDtypeStruct((b, l, d), F32)] + cast_shapes,
        grid=(b, tiles),
        in_specs=in_specs + cast_in,
        out_specs=[pl.BlockSpec((None, tl, d), lambda i, j: (i, j, 0))] + cast_out,
        scratch_shapes=[
            pltpu.VMEM((d_inner // LANES, tl, LANES), F32),
            pltpu.VMEM((conv_dim // LANES, tl, LANES), F32),
            pltpu.VMEM((1, tl, LANES), F32),
            pltpu.VMEM((SUBLANES, conv_dim), F32),
            pltpu.VMEM((tl, d_inner), BF16),
            pltpu.VMEM((d_inner, M_STATE), F32),
        ],
        compiler_params=pltpu.CompilerParams(
            dimension_semantics=("arbitrary", "arbitrary"), vmem_limit_bytes=VMEM_LIMIT),
        name="mamba",
    )(x, nw.reshape(1, d), perm, w_in, wdt, conv_w, conv_b.reshape(1, conv_dim), dtb, alog, dexp,
      gnw.reshape(1, d_inner), w_out, *[arr for arr, _ in cast_jobs])


def _boundary_distance(p, b, sign):
    n, w = p.shape
    span = 2 * b
    if b >= SUBLANES:
        parts = []
        for i in range(n // span):
            ref = p[i * span + b - 1:i * span + b, :]
            parts += [ref - p[i * span:i * span + b, :], p[i * span + b:(i + 1) * span, :] - ref]
        return jnp.concatenate(parts, axis=0)
    p3 = p.reshape(n // SUBLANES, SUBLANES, w)
    sub = lax.broadcasted_iota(jnp.int32, p3.shape, 1)
    ref = None
    for i in range(SUBLANES // span):
        m = i * span + b - 1
        piece = jnp.broadcast_to(p3[:, m:m + 1, :], p3.shape)
        ref = piece if ref is None else jnp.where(sub >= i * span, piece, ref)
    return (p - ref.reshape(n, w)) * sign


def _hgrn_kernel(x_ref, nw_ref, win_ref, lbl_ref, hnw_ref, wout_ref, o_ref,
                 proj_ref, obf_ref, st_ref, *, tl, layer):
    d_key = lbl_ref.shape[1]
    n_heads = d_key // H_HEAD
    f_off, v_off, g_off = d_key, 2 * d_key, 2 * d_key + n_heads * H_HEAD

    @pl.when(pl.program_id(1) == 0)
    def _():
        st_ref[...] = jnp.zeros_like(st_ref)

    x = x_ref[...]
    u = _rms(x, nw_ref[...]).astype(BF16)
    proj_ref[...] = _dot(u, win_ref[...])

    logits = lbl_ref[...]
    e = jnp.exp(logits - jnp.max(logits, axis=0, keepdims=True))
    lb = jnp.sum(e[1:layer + 1, :], axis=0, keepdims=True) / jnp.sum(e, axis=0, keepdims=True)

    row = lax.broadcasted_iota(jnp.int32, (CHUNK, CHUNK), 0)
    col = lax.broadcasted_iota(jnp.int32, (CHUNK, CHUNK), 1)
    tril_bf = jnp.where(row >= col, 1.0, 0.0).astype(BF16)
    levels = []
    k = 0
    while (1 << k) < CHUNK:
        right = ((row >> k) & 1) == 1
        pair = right & (((col >> k) & 1) == 0) & ((row >> (k + 1)) == (col >> (k + 1)))
        levels.append((k, right, jnp.where(right, 1.0, -1.0), pair))
        k += 1

    def chunk_body(c, carry):
        r0 = pl.multiple_of(c * CHUNK, CHUNK)
        rows = pl.ds(r0, CHUNK)
        forget = lb + (1.0 - lb) * jax.nn.sigmoid(proj_ref[rows, f_off:f_off + d_key])
        p_all = _cumsum_rows(jnp.log2(forget), tril_bf)
        for h in range(n_heads):
            hc = slice(h * H_HEAD, (h + 1) * H_HEAD)
            q = _silu(proj_ref[rows, h * H_HEAD:(h + 1) * H_HEAD])
            fg = forget[:, hc]
            kk = 1.0 - fg
            v_bf = proj_ref[rows, v_off + h * H_HEAD:v_off + (h + 1) * H_HEAD].astype(BF16)
            p = p_all[:, hc]
            s_in = st_ref[h]
            q_bf = q.astype(BF16)
            kk_bf = kk.astype(BF16)
            a = jnp.where(row == col, _dot_nt(q_bf, kk_bf), 0.0)
            for (lg, right, sign, pair) in levels:
                if lg == 0:
                    e_bf = jnp.where(right, fg, 1.0).astype(BF16)
                else:
                    e_bf = jnp.exp2(_boundary_distance(p, 1 << lg, sign)).astype(BF16)
                a = jnp.where(pair, _dot_nt(q_bf * e_bf, kk_bf * e_bf), a)
            o = _dot(jnp.concatenate([a.astype(BF16), (q * jnp.exp2(p)).astype(BF16)], axis=1),
                     jnp.concatenate([v_bf, s_in.astype(BF16)], axis=0))
            p_last = p[CHUNK - 1:CHUNK, :]
            k_end = kk * jnp.exp2(p_last - p)
            decay_col = jnp.broadcast_to(jnp.exp2(p_last), (CHUNK, H_HEAD)).T
            st_ref[h] = decay_col * s_in + _dot(k_end.T.astype(BF16), v_bf)
            o = o * lax.rsqrt(jnp.mean(o * o, axis=-1, keepdims=True) + EPS) * hnw_ref[...]
            o = o * _silu(proj_ref[rows, g_off + h * H_HEAD:g_off + (h + 1) * H_HEAD])
            obf_ref[rows, hc] = o.astype(BF16)
        return carry

    lax.fori_loop(0, tl // CHUNK, chunk_body, 0, unroll=True)
    o_ref[...] = x + _dot(obf_ref[...], wout_ref[...])


def _hgrn(x, nw, w_in, w_out, lb_logits, hnw, cast_jobs, *, layer, tl=512):
    b, l, d = x.shape
    depth, d_key = lb_logits.shape
    d_proj = w_in.shape[-1]
    d_val = w_out.shape[-2]
    n_heads = d_key // H_HEAD
    tiles = l // tl
    cast_in, cast_out, cast_shapes = _cast_plumbing(cast_jobs, b * tiles, lambda i, j: i * tiles + j)
    in_specs = [
        pl.BlockSpec((None, tl, d), lambda i, j: (i, j, 0)),
        _const_spec((1, d)),
        _const_spec((d, d_proj)),
        _const_spec((depth, d_key)),
        _const_spec((1, H_HEAD)),
        _const_spec((d_val, d)),
    ]
    return pl.pallas_call(
        _with_casts(functools.partial(_hgrn_kernel, tl=tl, layer=layer), len(in_specs), len(cast_jobs)),
        out_shape=[jax.ShapeDtypeStruct((b, l, d), F32)] + cast_shapes,
        grid=(b, tiles),
        in_specs=in_specs + cast_in,
        out_specs=[pl.BlockSpec((None, tl, d), lambda i, j: (i, j, 0))] + cast_out,
        scratch_shapes=[
            pltpu.VMEM((tl, d_proj), F32),
            pltpu.VMEM((tl, d_val), BF16),
            pltpu.VMEM((n_heads, H_HEAD, H_HEAD), F32),
        ],
        compiler_params=pltpu.CompilerParams(
            dimension_semantics=("arbitrary", "arbitrary"), vmem_limit_bytes=VMEM_LIMIT),
        name="hgrn",
    )(x, nw.reshape(1, d), w_in, lb_logits, hnw.reshape(1, H_HEAD), w_out, *[arr for arr, _ in cast_jobs])


def kernel(x, norm_w, ffn_w_gate, ffn_w_up, ffn_w_down, m_w_in, m_conv_w, m_conv_b, m_dt_bias, m_a_log,
           m_d, m_norm_w, m_w_out, h_w_in, h_lb_logits, h_norm_w, h_w_out, final_norm_w):
    b, l, d = x.shape
    depth = norm_w.shape[0]
    d_ff = ffn_w_gate.shape[-1]
    fnw = final_norm_w.reshape(1, d)

    m_aligned = (m_w_in.shape[-1] // LANES) * LANES

    def cast_jobs(kind, i, j=0):
        if kind == "ffn":
            return [(ffn_w_gate, (i, j)), (ffn_w_up, (i, j)), (ffn_w_down, (i, j))]
        if kind == "mamba":
            return [(m_w_out, (i,))]
        return [(h_w_in, (i,)), (h_w_out, (i,))]

    def weights(kind, i, casts):
        if kind == "mamba":
            return m_w_in[i, :, :m_aligned].astype(BF16), m_w_in[i, :, m_aligned:].astype(BF16), casts[0]
        return tuple(casts)

    calls = []
    for i in range(depth):
        calls += [("ffn", i, 0), ("mamba" if i % 2 == 0 else "hgrn", i // 2, 0), ("ffn", i, 1)]

    w = weights("ffn", 0, [a[lead].astype(BF16) for a, lead in cast_jobs("ffn", 0, 0)])
    for n, (kind, i, j) in enumerate(calls):
        jobs = cast_jobs(*calls[n + 1]) if n + 1 < len(calls) else []
        if kind == "ffn":
            out = _ffn(x.reshape(b * l, d), norm_w[i, 2 * j].reshape(1, d), *w, fnw, jobs,
                       final=(n == len(calls) - 1))
            x = out[0].reshape(b, l, d)
        elif kind == "mamba":
            out = _mamba(x, norm_w[2 * i, 1], *w, m_conv_w[i], m_conv_b[i], m_dt_bias[i], m_a_log[i],
                         m_d[i], m_norm_w[i], jobs)
            x = out[0]
        else:
            out = _hgrn(x, norm_w[2 * i + 1, 1], *w, h_lb_logits, h_norm_w[i], jobs, layer=2 * i + 1)
            x = out[0]
        if jobs:
            w = weights(calls[n + 1][0], calls[n + 1][1], out[1:])
    return x
```

```python
import functools

import jax
import jax.numpy as jnp
from jax import lax
from jax.experimental import pallas as pl
from jax.experimental.pallas import tpu as pltpu

EPS = 1e-6
LOG2E = 1.4426950408889634
F32 = jnp.float32
BF16 = jnp.bfloat16

M_HEADDIM = 64
M_GROUPS = 4
M_STATE = 128
M_CONV = 4
H_HEAD = 128

CHUNK = 128
SUBLANES = 8
LANES = 128
COL_BLOCK = 256
VMEM_LIMIT = 56 * 1024 * 1024


def _dot(a, b):
    return jnp.dot(a, b, preferred_element_type=F32)


def _dot_nt(a, b):
    return lax.dot_general(a, b, (((1,), (1,)), ((), ())), preferred_element_type=F32)


def _rms(x, w):
    return x * lax.rsqrt(jnp.mean(x * x, axis=-1, keepdims=True) + EPS) * w


def _silu(x):
    return x * jax.nn.sigmoid(x)


def _cumsum_rows(x, tril_bf):
    hi = x.astype(BF16)
    lo = (x - hi.astype(F32)).astype(BF16)
    return _dot(tril_bf, hi) + _dot(tril_bf, lo)


def _const_spec(shape, lead=()):
    block = (None,) * len(lead) + tuple(shape)
    index = tuple(lead) + (0,) * len(shape)
    return pl.BlockSpec(block, lambda *_: index, pipeline_mode=pl.Buffered(1))


def _ffn_kernel(x_ref, nw_ref, wg_ref, wu_ref, wd_ref, fnw_ref, o_ref, h_ref, *, f_chunk, final):
    x = x_ref[...]
    xn = _rms(x, nw_ref[...]).astype(BF16)
    d_ff = wg_ref.shape[1]
    for c in range(d_ff // f_chunk):
        sl = slice(c * f_chunk, (c + 1) * f_chunk)
        g = _dot(xn, wg_ref[:, sl])
        u = _dot(xn, wu_ref[:, sl])
        h_ref[:, sl] = (_silu(g) * u).astype(BF16)
    y = x + 0.5 * _dot(h_ref[...], wd_ref[...])
    if final:
        y = _rms(y, fnw_ref[...])
    o_ref[...] = y


def _ffn(x2d, nw, wg, wu, wd, fnw, *, which, final, tm=1024, f_chunk=256):
    t, d = x2d.shape
    d_ff = wg.shape[-1]
    return pl.pallas_call(
        functools.partial(_ffn_kernel, f_chunk=f_chunk, final=final),
        out_shape=jax.ShapeDtypeStruct((t, d), F32),
        grid=(t // tm,),
        in_specs=[
            pl.BlockSpec((tm, d), lambda i: (i, 0)),
            _const_spec((1, d)),
            _const_spec((d, d_ff), which),
            _const_spec((d, d_ff), which),
            _const_spec((d_ff, d), which),
            _const_spec((1, d)),
        ],
        out_specs=pl.BlockSpec((tm, d), lambda i: (i, 0)),
        scratch_shapes=[pltpu.VMEM((tm, d_ff), BF16)],
        compiler_params=pltpu.CompilerParams(
            dimension_semantics=("arbitrary",), vmem_limit_bytes=VMEM_LIMIT),
        name="ffn_final" if final else "ffn",
    )(x2d, nw, wg, wu, wd, fnw)


def _lane_bcast(col):
    return jnp.broadcast_to(col, (col.shape[0], 128))


def _mamba_kernel(x_ref, nw_ref, perm_ref, win_ref, wdt_ref, cw_ref, cb_ref, dtb_ref, alog_ref,
                  dexp_ref, gnw_ref, wout_ref, o_ref,
                  z_ref, xc_ref, dt_ref, tail_ref, ybf_ref, st_ref, out_ref, *, tl):
    d_inner = z_ref.shape[0] * LANES
    conv_dim = xc_ref.shape[0] * LANES
    n_pairs = d_inner // LANES
    pairs_per_group = n_pairs // M_GROUPS
    gwidth = d_inner // M_GROUPS
    b_blk = d_inner // LANES
    c_blk = b_blk + M_GROUPS * M_STATE // LANES
    seg = CHUNK // SUBLANES
    n_chunks = tl // CHUNK

    @pl.when(pl.program_id(1) == 0)
    def _():
        st_ref[...] = jnp.zeros_like(st_ref)
        tail_ref[...] = jnp.zeros_like(tail_ref)

    def store_blocks(ref, first_block, val):
        for q in range(val.shape[1] // LANES):
            ref[first_block + q] = val[:, q * LANES:(q + 1) * LANES]

    un = _rms(x_ref[...], nw_ref[...]).astype(BF16)
    u = jnp.concatenate([_dot(perm_ref[...], un[c * CHUNK:(c + 1) * CHUNK, :]) for c in range(n_chunks)],
                        axis=0).astype(BF16)
    store_blocks(dt_ref, 0, _dot(u, wdt_ref[...]))
    first_sublane = lax.broadcasted_iota(jnp.int32, (SUBLANES, COL_BLOCK), 0) == 0
    n_xbc_blocks = conv_dim // COL_BLOCK
    n_z_blocks = d_inner // COL_BLOCK
    lane_blocks = COL_BLOCK // LANES
    for j in range(max(n_xbc_blocks, n_z_blocks)):
        cols = slice(j * COL_BLOCK, (j + 1) * COL_BLOCK)
        if j < n_xbc_blocks:
            xr = _dot(u, win_ref[:, d_inner + j * COL_BLOCK:d_inner + (j + 1) * COL_BLOCK])
        if j < n_z_blocks:
            store_blocks(z_ref, j * lane_blocks, _dot(u, win_ref[:, cols]))
        if j < n_xbc_blocks:
            outs = []
            for c in range(n_chunks):
                xc_c = xr[c * CHUNK:(c + 1) * CHUNK, :]
                history = []
                for k in range(M_CONV - 1, 0, -1):
                    group = xc_c[(seg - k) * SUBLANES:(seg - k + 1) * SUBLANES, :]
                    if c == 0:
                        before = tail_ref[k - 1:k, cols]
                    else:
                        r = c * CHUNK - k * SUBLANES + SUBLANES - 1
                        before = xr[r:r + 1, :]
                    history.append(jnp.where(first_sublane, before, pltpu.roll(group, 1, axis=0)))
                xext = jnp.concatenate(history + [xc_c], axis=0)
                acc = cb_ref[:, cols]
                for k in range(M_CONV):
                    acc = acc + cw_ref[k:k + 1, cols] * xext[k * SUBLANES:k * SUBLANES + CHUNK, :]
                outs.append(_silu(acc))
            for k in range(1, M_CONV):
                tail_ref[k - 1:k, cols] = xr[tl - k * SUBLANES + SUBLANES - 1:tl - k * SUBLANES + SUBLANES, :]
            store_blocks(xc_ref, j * lane_blocks, jnp.concatenate(outs, axis=0))

    a_neg2 = -jnp.exp(alog_ref[...]) * LOG2E
    row = lax.broadcasted_iota(jnp.int32, (CHUNK, CHUNK), 0)
    col = lax.broadcasted_iota(jnp.int32, (CHUNK, CHUNK), 1)

    def token_of(r):
        return (r & (SUBLANES - 1)) * seg + (r >> 3)

    tril = token_of(row) >= token_of(col)
    tril_bf = jnp.where(tril, 1.0, 0.0).astype(BF16)
    first_lanes = col < M_HEADDIM
    first_rows = row < M_HEADDIM

    def chunk_body(c, carry):
        r0 = pl.multiple_of(c * CHUNK, CHUNK)
        rows = pl.ds(r0, CHUNK)
        dt_raw = dt_ref[0, rows, :] + dtb_ref[...]
        dt = jnp.maximum(dt_raw, 0.0) + jnp.log1p(jnp.exp(-jnp.abs(dt_raw)))
        cs = _cumsum_rows(dt * a_neg2, tril_bf)
        cs_t = cs.T
        dt_t = dt.T
        cs_last_t = cs_t[:, CHUNK - 1:CHUNK]
        w_t = jnp.exp2(cs_last_t - cs_t) * dt_t
        cd_t = jnp.exp2(cs_last_t)
        src_t = cs_t - jnp.log2(dt_t)

        for g in range(M_GROUPS):
            bg_bf = xc_ref[b_blk + g, rows, :].astype(BF16)
            cg_bf = xc_ref[c_blk + g, rows, :].astype(BF16)
            cb = _dot_nt(cg_bf, bg_bf)
            grows = slice(g * gwidth, (g + 1) * gwidth)
            y_off = _dot_nt(cg_bf, st_ref[grows, :].astype(BF16))
            y_parts = []
            for jj in range(pairs_per_group):
                j = g * pairs_per_group + jj
                h0 = 2 * j
                pcols = slice(j * 128, (j + 1) * 128)
                xs = xc_ref[j, rows, :]
                xs_bf = xs.astype(BF16)
                yd, ecol = [], []
                for h in (h0, h0 + 1):
                    c_col = _lane_bcast(cs[:, h:h + 1])
                    lmat = jnp.exp2(jnp.where(tril, c_col - src_t[h:h + 1, :], -jnp.inf))
                    m = (cb * lmat).astype(BF16)
                    yd.append(_dot(m, xs_bf))
                    ecol.append(jnp.exp2(c_col))
                y = jnp.where(first_lanes, yd[0], yd[1])
                y = y + jnp.where(first_lanes, ecol[0], ecol[1]) * y_off[:, jj * 128:(jj + 1) * 128]
                y_parts.append(y + dexp_ref[:, pcols] * xs)
                w_rows = jnp.where(first_rows, w_t[h0:h0 + 1, :], w_t[h0 + 1:h0 + 2, :])
                cd_rows = jnp.where(first_rows, cd_t[h0:h0 + 1, :], cd_t[h0 + 1:h0 + 2, :])
                xw_t = (xs.T * w_rows).astype(BF16)
                st_ref[pcols, :] = cd_rows * st_ref[pcols, :] + _dot(xw_t, bg_bf)
            zg = jnp.concatenate([z_ref[g * pairs_per_group + jj, rows, :] for jj in range(pairs_per_group)], axis=1)
            yg = jnp.concatenate(y_parts, axis=1) * _silu(zg)
            yg = yg * lax.rsqrt(jnp.mean(yg * yg, axis=-1, keepdims=True) + EPS)
            ybf_ref[rows, grows] = (yg * gnw_ref[:, grows]).astype(BF16)
        return carry

    lax.fori_loop(0, tl // CHUNK, chunk_body, 0, unroll=True)
    store_blocks(out_ref, 0, _dot(ybf_ref[...], wout_ref[...]))
    groups = []
    for c in range(n_chunks):
        for s8 in range(SUBLANES):
            for half in range(seg // SUBLANES):
                start = c * CHUNK + half * SUBLANES * SUBLANES + s8
                groups.append(jnp.concatenate(
                    [out_ref[q, pl.ds(start, SUBLANES, stride=SUBLANES), :] for q in range(out_ref.shape[0])], axis=1))
    o_ref[...] = x_ref[...] + jnp.concatenate(groups, axis=0)


def _mamba(x, nw, w_in, w_out, conv_w, conv_b, dt_bias, a_log, d_skip, gnw, *, which, tl=512):
    b, l, d = x.shape
    n_heads = dt_bias.shape[0]
    d_inner = n_heads * M_HEADDIM
    conv_dim = d_inner + 2 * M_GROUPS * M_STATE
    d_proj = w_in.shape[-1]
    pad = 128 - n_heads
    wdt = jnp.pad(w_in[which, :, d_inner + conv_dim:], ((0, 0), (0, pad)))
    dtb = jnp.pad(dt_bias, (0, pad)).reshape(1, 128)
    alog = jnp.pad(a_log, (0, pad)).reshape(1, 128)
    dexp = jnp.repeat(d_skip, M_HEADDIM).reshape(1, d_inner)
    r = jnp.arange(CHUNK)
    perm = (jnp.arange(CHUNK)[None, :] == ((r % SUBLANES) * (CHUNK // SUBLANES) + r // SUBLANES)[:, None]).astype(BF16)
    return pl.pallas_call(
        functools.partial(_mamba_kernel, tl=tl),
        out_shape=jax.ShapeDtypeStruct((b, l, d), F32),
        grid=(b, l // tl),
        in_specs=[
            pl.BlockSpec((None, tl, d), lambda i, j: (i, j, 0)),
            _const_spec((1, d)),
            _const_spec((CHUNK, CHUNK)),
            _const_spec((d, d_proj), (which,)),
            _const_spec((d, 128)),
            _const_spec((M_CONV, conv_dim)),
            _const_spec((1, conv_dim)),
            _const_spec((1, 128)),
            _const_spec((1, 128)),
            _const_spec((1, d_inner)),
            _const_spec((1, d_inner)),
            _const_spec((d_inner, d), (which,)),
        ],
        out_specs=pl.BlockSpec((None, tl, d), lambda i, j: (i, j, 0)),
        scratch_shapes=[
            pltpu.VMEM((d_inner // LANES, tl, LANES), F32),
            pltpu.VMEM((conv_dim // LANES, tl, LANES), F32),
            pltpu.VMEM((1, tl, LANES), F32),
            pltpu.VMEM((SUBLANES, conv_dim), F32),
            pltpu.VMEM((tl, d_inner), BF16),
            pltpu.VMEM((d_inner, M_STATE), F32),
            pltpu.VMEM((d // LANES, tl, LANES), F32),
        ],
        compiler_params=pltpu.CompilerParams(
            dimension_semantics=("arbitrary", "arbitrary"), vmem_limit_bytes=VMEM_LIMIT),
        name="mamba",
    )(x, nw.reshape(1, d), perm, w_in, wdt, conv_w, conv_b.reshape(1, conv_dim), dtb, alog, dexp,
      gnw.reshape(1, d_inner), w_out)


def _boundary_distance(p, b, sign):
    n, w = p.shape
    span = 2 * b
    if b >= SUBLANES:
        parts = []
        for i in range(n // span):
            ref = p[i * span + b - 1:i * span + b, :]
            parts += [ref - p[i * span:i * span + b, :], p[i * span + b:(i + 1) * span, :] - ref]
        return jnp.concatenate(parts, axis=0)
    p3 = p.reshape(n // SUBLANES, SUBLANES, w)
    sub = lax.broadcasted_iota(jnp.int32, p3.shape, 1)
    ref = None
    for i in range(SUBLANES // span):
        m = i * span + b - 1
        piece = jnp.broadcast_to(p3[:, m:m + 1, :], p3.shape)
        ref = piece if ref is None else jnp.where(sub >= i * span, piece, ref)
    return (p - ref.reshape(n, w)) * sign


def _hgrn_kernel(x_ref, nw_ref, win_ref, lbl_ref, hnw_ref, wout_ref, o_ref,
                 proj_ref, obf_ref, st_ref, *, tl, layer):
    d_key = lbl_ref.shape[1]
    n_heads = d_key // H_HEAD
    f_off, v_off, g_off = d_key, 2 * d_key, 2 * d_key + n_heads * H_HEAD

    @pl.when(pl.program_id(1) == 0)
    def _():
        st_ref[...] = jnp.zeros_like(st_ref)

    x = x_ref[...]
    u = _rms(x, nw_ref[...]).astype(BF16)
    proj_ref[...] = _dot(u, win_ref[...])

    logits = lbl_ref[...]
    e = jnp.exp(logits - jnp.max(logits, axis=0, keepdims=True))
    lb = jnp.sum(e[1:layer + 1, :], axis=0, keepdims=True) / jnp.sum(e, axis=0, keepdims=True)

    row = lax.broadcasted_iota(jnp.int32, (CHUNK, CHUNK), 0)
    col = lax.broadcasted_iota(jnp.int32, (CHUNK, CHUNK), 1)
    tril_bf = jnp.where(row >= col, 1.0, 0.0).astype(BF16)
    levels = []
    k = 0
    while (1 << k) < CHUNK:
        right = ((row >> k) & 1) == 1
        pair = right & (((col >> k) & 1) == 0) & ((row >> (k + 1)) == (col >> (k + 1)))
        levels.append((k, right, jnp.where(right, 1.0, -1.0), pair))
        k += 1

    def chunk_body(c, carry):
        r0 = pl.multiple_of(c * CHUNK, CHUNK)
        rows = pl.ds(r0, CHUNK)
        forget = lb + (1.0 - lb) * jax.nn.sigmoid(proj_ref[rows, f_off:f_off + d_key])
        p_all = _cumsum_rows(jnp.log2(forget), tril_bf)
        for h in range(n_heads):
            hc = slice(h * H_HEAD, (h + 1) * H_HEAD)
            q = _silu(proj_ref[rows, h * H_HEAD:(h + 1) * H_HEAD])
            fg = forget[:, hc]
            kk = 1.0 - fg
            v_bf = proj_ref[rows, v_off + h * H_HEAD:v_off + (h + 1) * H_HEAD].astype(BF16)
            p = p_all[:, hc]
            s_in = st_ref[h]
            q_bf = q.astype(BF16)
            kk_bf = kk.astype(BF16)
            a = jnp.where(row == col, _dot_nt(q_bf, kk_bf), 0.0)
            for (lg, right, sign, pair) in levels:
                if lg == 0:
                    e_bf = jnp.where(right, fg, 1.0).astype(BF16)
                else:
                    e_bf = jnp.exp2(_boundary_distance(p, 1 << lg, sign)).astype(BF16)
                a = jnp.where(pair, _dot_nt(q_bf * e_bf, kk_bf * e_bf), a)
            o = _dot(jnp.concatenate([a.astype(BF16), (q * jnp.exp2(p)).astype(BF16)], axis=1),
                     jnp.concatenate([v_bf, s_in.astype(BF16)], axis=0))
            p_last = p[CHUNK - 1:CHUNK, :]
            k_end = kk * jnp.exp2(p_last - p)
            decay_col = jnp.broadcast_to(jnp.exp2(p_last), (CHUNK, H_HEAD)).T
            st_ref[h] = decay_col * s_in + _dot(k_end.T.astype(BF16), v_bf)
            o = o * lax.rsqrt(jnp.mean(o * o, axis=-1, keepdims=True) + EPS) * hnw_ref[...]
            o = o * _silu(proj_ref[rows, g_off + h * H_HEAD:g_off + (h + 1) * H_HEAD])
            obf_ref[rows, hc] = o.astype(BF16)
        return carry

    lax.fori_loop(0, tl // CHUNK, chunk_body, 0, unroll=True)
    o_ref[...] = x + _dot(obf_ref[...], wout_ref[...])


def _hgrn(x, nw, w_in, w_out, lb_logits, hnw, *, which, layer, tl=512):
    b, l, d = x.shape
    depth, d_key = lb_logits.shape
    d_proj = w_in.shape[-1]
    d_val = w_out.shape[-2]
    n_heads = d_key // H_HEAD
    return pl.pallas_call(
        functools.partial(_hgrn_kernel, tl=tl, layer=layer),
        out_shape=jax.ShapeDtypeStruct((b, l, d), F32),
        grid=(b, l // tl),
        in_specs=[
            pl.BlockSpec((None, tl, d), lambda i, j: (i, j, 0)),
            _const_spec((1, d)),
            _const_spec((d, d_proj), (which,)),
            _const_spec((depth, d_key)),
            _const_spec((1, H_HEAD)),
            _const_spec((d_val, d), (which,)),
        ],
        out_specs=pl.BlockSpec((None, tl, d), lambda i, j: (i, j, 0)),
        scratch_shapes=[
            pltpu.VMEM((tl, d_proj), F32),
            pltpu.VMEM((tl, d_val), BF16),
            pltpu.VMEM((n_heads, H_HEAD, H_HEAD), F32),
        ],
        compiler_params=pltpu.CompilerParams(
            dimension_semantics=("arbitrary", "arbitrary"), vmem_limit_bytes=VMEM_LIMIT),
        name="hgrn",
    )(x, nw.reshape(1, d), w_in, lb_logits, hnw.reshape(1, H_HEAD), w_out)


def kernel(x, norm_w, ffn_w_gate, ffn_w_up, ffn_w_down, m_w_in, m_conv_w, m_conv_b, m_dt_bias, m_a_log,
           m_d, m_norm_w, m_w_out, h_w_in, h_lb_logits, h_norm_w, h_w_out, final_norm_w):
    b, l, d = x.shape
    depth = norm_w.shape[0]
    fnw = final_norm_w.reshape(1, d)
    wg, wu, wd = ffn_w_gate.astype(BF16), ffn_w_up.astype(BF16), ffn_w_down.astype(BF16)
    m_in, m_out = m_w_in.astype(BF16), m_w_out.astype(BF16)
    h_in, h_out = h_w_in.astype(BF16), h_w_out.astype(BF16)

    def ffn(x, i, j, final=False):
        y = _ffn(x.reshape(b * l, d), norm_w[i, 2 * j].reshape(1, d), wg, wu, wd, fnw,
                 which=(i, j), final=final)
        return y.reshape(b, l, d)

    for i in range(depth):
        x = ffn(x, i, 0)
        j = i // 2
        if i % 2 == 0:
            x = _mamba(x, norm_w[i, 1], m_in, m_out, m_conv_w[j], m_conv_b[j], m_dt_bias[j], m_a_log[j],
                       m_d[j], m_norm_w[j], which=j)
        else:
            x = _hgrn(x, norm_w[i, 1], h_in, h_out, h_lb_logits, h_norm_w[j], which=j, layer=i)
        x = ffn(x, i, 1, final=(i == depth - 1))
    return x
```

```python
import functools

import jax
import jax.numpy as jnp
from jax import lax
from jax.experimental import pallas as pl
from jax.experimental.pallas import tpu as pltpu

EPS = 1e-6
LOG2E = 1.4426950408889634
F32 = jnp.float32
BF16 = jnp.bfloat16

M_HEADDIM = 64
M_GROUPS = 4
M_STATE = 128
M_CONV = 4
H_HEAD = 128

CHUNK = 128
SUBLANES = 8
LANES = 128
COL_BLOCK = 256
VMEM_LIMIT = 56 * 1024 * 1024


def _dot(a, b):
    return jnp.dot(a, b, preferred_element_type=F32)


def _dot_nt(a, b):
    return lax.dot_general(a, b, (((1,), (1,)), ((), ())), preferred_element_type=F32)


def _rms(x, w):
    return x * lax.rsqrt(jnp.mean(x * x, axis=-1, keepdims=True) + EPS) * w


def _silu(x):
    return x * jax.nn.sigmoid(x)


def _cumsum_rows(x, tril_bf):
    hi = x.astype(BF16)
    lo = (x - hi.astype(F32)).astype(BF16)
    return _dot(tril_bf, hi) + _dot(tril_bf, lo)


def _const_spec(shape, lead=()):
    block = (None,) * len(lead) + tuple(shape)
    index = tuple(lead) + (0,) * len(shape)
    return pl.BlockSpec(block, lambda *_: index, pipeline_mode=pl.Buffered(1))


def _ffn_kernel(x_ref, nw_ref, wg_ref, wu_ref, wd_ref, fnw_ref, o_ref, h_ref, *, f_chunk, final):
    x = x_ref[...]
    xn = _rms(x, nw_ref[...]).astype(BF16)
    d_ff = wg_ref.shape[1]
    for c in range(d_ff // f_chunk):
        sl = slice(c * f_chunk, (c + 1) * f_chunk)
        g = _dot(xn, wg_ref[:, sl])
        u = _dot(xn, wu_ref[:, sl])
        h_ref[:, sl] = (_silu(g) * u).astype(BF16)
    y = x + 0.5 * _dot(h_ref[...], wd_ref[...])
    if final:
        y = _rms(y, fnw_ref[...])
    o_ref[...] = y


def _ffn(x2d, nw, wg, wu, wd, fnw, *, which, final, tm=1024, f_chunk=256):
    t, d = x2d.shape
    d_ff = wg.shape[-1]
    return pl.pallas_call(
        functools.partial(_ffn_kernel, f_chunk=f_chunk, final=final),
        out_shape=jax.ShapeDtypeStruct((t, d), F32),
        grid=(t // tm,),
        in_specs=[
            pl.BlockSpec((tm, d), lambda i: (i, 0)),
            _const_spec((1, d)),
            _const_spec((d, d_ff), which),
            _const_spec((d, d_ff), which),
            _const_spec((d_ff, d), which),
            _const_spec((1, d)),
        ],
        out_specs=pl.BlockSpec((tm, d), lambda i: (i, 0)),
        scratch_shapes=[pltpu.VMEM((tm, d_ff), BF16)],
        compiler_params=pltpu.CompilerParams(
            dimension_semantics=("arbitrary",), vmem_limit_bytes=VMEM_LIMIT),
        name="ffn_final" if final else "ffn",
    )(x2d, nw, wg, wu, wd, fnw)


def _lane_bcast(col):
    return jnp.broadcast_to(col, (col.shape[0], 128))


def _mamba_kernel(x_ref, nw_ref, perm_ref, win_ref, wdt_ref, cw_ref, cb_ref, dtb_ref, alog_ref,
                  dexp_ref, gnw_ref, wout_ref, o_ref,
                  z_ref, xc_ref, dt_ref, tail_ref, ybf_ref, st_ref, out_ref, *, tl):
    d_inner = z_ref.shape[0] * LANES
    conv_dim = xc_ref.shape[0] * LANES
    n_pairs = d_inner // LANES
    pairs_per_group = n_pairs // M_GROUPS
    gwidth = d_inner // M_GROUPS
    b_blk = d_inner // LANES
    c_blk = b_blk + M_GROUPS * M_STATE // LANES
    seg = CHUNK // SUBLANES
    n_chunks = tl // CHUNK

    @pl.when(pl.program_id(1) == 0)
    def _():
        st_ref[...] = jnp.zeros_like(st_ref)
        tail_ref[...] = jnp.zeros_like(tail_ref)

    def store_blocks(ref, first_block, val):
        for q in range(val.shape[1] // LANES):
            ref[first_block + q] = val[:, q * LANES:(q + 1) * LANES]

    un = _rms(x_ref[...], nw_ref[...]).astype(BF16)
    u = jnp.concatenate([_dot(perm_ref[...], un[c * CHUNK:(c + 1) * CHUNK, :]) for c in range(n_chunks)],
                        axis=0).astype(BF16)
    store_blocks(dt_ref, 0, _dot(u, wdt_ref[...]))
    first_sublane = lax.broadcasted_iota(jnp.int32, (SUBLANES, COL_BLOCK), 0) == 0
    n_xbc_blocks = conv_dim // COL_BLOCK
    n_z_blocks = d_inner // COL_BLOCK
    lane_blocks = COL_BLOCK // LANES
    for j in range(max(n_xbc_blocks, n_z_blocks)):
        cols = slice(j * COL_BLOCK, (j + 1) * COL_BLOCK)
        if j < n_xbc_blocks:
            xr = _dot(u, win_ref[:, d_inner + j * COL_BLOCK:d_inner + (j + 1) * COL_BLOCK])
        if j < n_z_blocks:
            store_blocks(z_ref, j * lane_blocks, _dot(u, win_ref[:, cols]))
        if j < n_xbc_blocks:
            outs = []
            for c in range(n_chunks):
                xc_c = xr[c * CHUNK:(c + 1) * CHUNK, :]
                history = []
                for k in range(M_CONV - 1, 0, -1):
                    group = xc_c[(seg - k) * SUBLANES:(seg - k + 1) * SUBLANES, :]
                    if c == 0:
                        before = tail_ref[k - 1:k, cols]
                    else:
                        r = c * CHUNK - k * SUBLANES + SUBLANES - 1
                        before = xr[r:r + 1, :]
                    history.append(jnp.where(first_sublane, before, pltpu.roll(group, 1, axis=0)))
                xext = jnp.concatenate(history + [xc_c], axis=0)
                acc = cb_ref[:, cols]
                for k in range(M_CONV):
                    acc = acc + cw_ref[k:k + 1, cols] * xext[k * SUBLANES:k * SUBLANES + CHUNK, :]
                outs.append(_silu(acc))
            for k in range(1, M_CONV):
                tail_ref[k - 1:k, cols] = xr[tl - k * SUBLANES + SUBLANES - 1:tl - k * SUBLANES + SUBLANES, :]
            store_blocks(xc_ref, j * lane_blocks, jnp.concatenate(outs, axis=0))

    a_neg2 = -jnp.exp(alog_ref[...]) * LOG2E
    row = lax.broadcasted_iota(jnp.int32, (CHUNK, CHUNK), 0)
    col = lax.broadcasted_iota(jnp.int32, (CHUNK, CHUNK), 1)

    def token_of(r):
        return (r & (SUBLANES - 1)) * seg + (r >> 3)

    tril = token_of(row) >= token_of(col)
    tril_bf = jnp.where(tril, 1.0, 0.0).astype(BF16)
    first_lanes = col < M_HEADDIM
    first_rows = row < M_HEADDIM

    def chunk_body(c, carry):
        r0 = pl.multiple_of(c * CHUNK, CHUNK)
        rows = pl.ds(r0, CHUNK)
        dt_raw = dt_ref[0, rows, :] + dtb_ref[...]
        dt = jnp.maximum(dt_raw, 0.0) + jnp.log1p(jnp.exp(-jnp.abs(dt_raw)))
        cs = _cumsum_rows(dt * a_neg2, tril_bf)
        cs_t = cs.T
        dt_t = dt.T
        cs_last_t = cs_t[:, CHUNK - 1:CHUNK]
        w_t = jnp.exp2(cs_last_t - cs_t) * dt_t
        cd_t = jnp.exp2(cs_last_t)
        src_t = cs_t - jnp.log2(dt_t)

        for g in range(M_GROUPS):
            bg_bf = xc_ref[b_blk + g, rows, :].astype(BF16)
            cg_bf = xc_ref[c_blk + g, rows, :].astype(BF16)
            cb = _dot_nt(cg_bf, bg_bf)
            grows = slice(g * gwidth, (g + 1) * gwidth)
            y_off = _dot_nt(cg_bf, st_ref[grows, :].astype(BF16))
            y_parts = []
            for jj in range(pairs_per_group):
                j = g * pairs_per_group + jj
                h0 = 2 * j
                pcols = slice(j * 128, (j + 1) * 128)
                xs = xc_ref[j, rows, :]
                xs_bf = xs.astype(BF16)
                yd, ecol = [], []
                for h in (h0, h0 + 1):
                    c_col = _lane_bcast(cs[:, h:h + 1])
                    lmat = jnp.exp2(jnp.where(tril, c_col - src_t[h:h + 1, :], -jnp.inf))
                    m = (cb * lmat).astype(BF16)
                    yd.append(_dot(m, xs_bf))
                    ecol.append(jnp.exp2(c_col))
                y = jnp.where(first_lanes, yd[0], yd[1])
                y = y + jnp.where(first_lanes, ecol[0], ecol[1]) * y_off[:, jj * 128:(jj + 1) * 128]
                y_parts.append(y + dexp_ref[:, pcols] * xs)
                w_rows = jnp.where(first_rows, w_t[h0:h0 + 1, :], w_t[h0 + 1:h0 + 2, :])
                cd_rows = jnp.where(first_rows, cd_t[h0:h0 + 1, :], cd_t[h0 + 1:h0 + 2, :])
                xw_t = (xs.T * w_rows).astype(BF16)
                st_ref[pcols, :] = cd_rows * st_ref[pcols, :] + _dot(xw_t, bg_bf)
            zg = jnp.concatenate([z_ref[g * pairs_per_group + jj, rows, :] for jj in range(pairs_per_group)], axis=1)
            yg = jnp.concatenate(y_parts, axis=1) * _silu(zg)
            yg = yg * lax.rsqrt(jnp.mean(yg * yg, axis=-1, keepdims=True) + EPS)
            ybf_ref[rows, grows] = (yg * gnw_ref[:, grows]).astype(BF16)
        return carry

    lax.fori_loop(0, tl // CHUNK, chunk_body, 0, unroll=True)
    store_blocks(out_ref, 0, _dot(ybf_ref[...], wout_ref[...]))
    groups = []
    for c in range(n_chunks):
        for s8 in range(SUBLANES):
            for half in range(seg // SUBLANES):
                start = c * CHUNK + half * SUBLANES * SUBLANES + s8
                groups.append(jnp.concatenate(
                    [out_ref[q, pl.ds(start, SUBLANES, stride=SUBLANES), :] for q in range(out_ref.shape[0])], axis=1))
    o_ref[...] = x_ref[...] + jnp.concatenate(groups, axis=0)


def _mamba(x, nw, w_in, w_out, conv_w, conv_b, dt_bias, a_log, d_skip, gnw, *, which, tl=512):
    b, l, d = x.shape
    n_heads = dt_bias.shape[0]
    d_inner = n_heads * M_HEADDIM
    conv_dim = d_inner + 2 * M_GROUPS * M_STATE
    d_proj = w_in.shape[-1]
    pad = 128 - n_heads
    wdt = jnp.pad(w_in[which, :, d_inner + conv_dim:], ((0, 0), (0, pad)))
    dtb = jnp.pad(dt_bias, (0, pad)).reshape(1, 128)
    alog = jnp.pad(a_log, (0, pad)).reshape(1, 128)
    dexp = jnp.repeat(d_skip, M_HEADDIM).reshape(1, d_inner)
    r = jnp.arange(CHUNK)
    perm = (jnp.arange(CHUNK)[None, :] == ((r % SUBLANES) * (CHUNK // SUBLANES) + r // SUBLANES)[:, None]).astype(BF16)
    return pl.pallas_call(
        functools.partial(_mamba_kernel, tl=tl),
        out_shape=jax.ShapeDtypeStruct((b, l, d), F32),
        grid=(b, l // tl),
        in_specs=[
            pl.BlockSpec((None, tl, d), lambda i, j: (i, j, 0)),
            _const_spec((1, d)),
            _const_spec((CHUNK, CHUNK)),
            _const_spec((d, d_proj), (which,)),
            _const_spec((d, 128)),
            _const_spec((M_CONV, conv_dim)),
            _const_spec((1, conv_dim)),
            _const_spec((1, 128)),
            _const_spec((1, 128)),
            _const_spec((1, d_inner)),
            _const_spec((1, d_inner)),
            _const_spec((d_inner, d), (which,)),
        ],
        out_specs=pl.BlockSpec((None, tl, d), lambda i, j: (i, j, 0)),
        scratch_shapes=[
            pltpu.VMEM((d_inner // LANES, tl, LANES), F32),
            pltpu.VMEM((conv_dim // LANES, tl, LANES), F32),
            pltpu.VMEM((1, tl, LANES), F32),
            pltpu.VMEM((SUBLANES, conv_dim), F32),
            pltpu.VMEM((tl, d_inner), BF16),
            pltpu.VMEM((d_inner, M_STATE), F32),
            pltpu.VMEM((d // LANES, tl, LANES), F32),
        ],
        compiler_params=pltpu.CompilerParams(
            dimension_semantics=("arbitrary", "arbitrary"), vmem_limit_bytes=VMEM_LIMIT),
        name="mamba",
    )(x, nw.reshape(1, d), perm, w_in, wdt, conv_w, conv_b.reshape(1, conv_dim), dtb, alog, dexp,
      gnw.reshape(1, d_inner), w_out)


def _boundary_distance(p, b, sign):
    n, w = p.shape
    span = 2 * b
    if b >= SUBLANES:
        parts = []
        for i in range(n // span):
            ref = p[i * span + b - 1:i * span + b, :]
            parts += [ref - p[i * span:i * span + b, :], p[i * span + b:(i + 1) * span, :] - ref]
        return jnp.concatenate(parts, axis=0)
    p3 = p.reshape(n // SUBLANES, SUBLANES, w)
    sub = lax.broadcasted_iota(jnp.int32, p3.shape, 1)
    ref = None
    for i in range(SUBLANES // span):
        m = i * span + b - 1
        piece = jnp.broadcast_to(p3[:, m:m + 1, :], p3.shape)
        ref = piece if ref is None else jnp.where(sub >= i * span, piece, ref)
    return (p - ref.reshape(n, w)) * sign


def _hgrn_kernel(xin_ref, xres_ref, nw_ref, win_ref, lbl_ref, hnw_ref, wout_ref, o_ref,
                 proj_a, proj_b, obf_ref, st_ref, *, tl, layer, tiles_per_seq):
    d_key = lbl_ref.shape[1]
    n_heads = d_key // H_HEAD
    d_proj = win_ref.shape[1]
    f_off, v_off, g_off = d_key, 2 * d_key, 2 * d_key + n_heads * H_HEAD
    n = pl.program_id(0)

    @pl.when(n == 0)
    def _():
        proj_b[...] = jnp.zeros_like(proj_b)

    @pl.when(lax.rem(n + tiles_per_seq - 1, tiles_per_seq) == 0)
    def _():
        st_ref[...] = jnp.zeros_like(st_ref)

    logits = lbl_ref[...]
    e = jnp.exp(logits - jnp.max(logits, axis=0, keepdims=True))
    lb = jnp.sum(e[1:layer + 1, :], axis=0, keepdims=True) / jnp.sum(e, axis=0, keepdims=True)

    row = lax.broadcasted_iota(jnp.int32, (CHUNK, CHUNK), 0)
    col = lax.broadcasted_iota(jnp.int32, (CHUNK, CHUNK), 1)
    tril_bf = jnp.where(row >= col, 1.0, 0.0).astype(BF16)
    levels = []
    k = 0
    while (1 << k) < CHUNK:
        right = ((row >> k) & 1) == 1
        pair = right & (((col >> k) & 1) == 0) & ((row >> (k + 1)) == (col >> (k + 1)))
        levels.append((k, right, jnp.where(right, 1.0, -1.0), pair))
        k += 1

    n_chunks = tl // CHUNK
    slots_per_block = (n_chunks * n_heads * COL_BLOCK) // d_proj

    def chunk_body(c, proj_ref, nxt_ref, u):
        rows = slice(c * CHUNK, (c + 1) * CHUNK)
        forget = lb + (1.0 - lb) * jax.nn.sigmoid(proj_ref[rows, f_off:f_off + d_key])
        p_all = _cumsum_rows(jnp.log2(forget), tril_bf)
        for h in range(n_heads):
            slot = c * n_heads + h
            if slot % slots_per_block == 0:
                cols = slice((slot // slots_per_block) * COL_BLOCK, (slot // slots_per_block + 1) * COL_BLOCK)
                nxt_ref[:, cols] = _dot(u, win_ref[:, cols])
            hc = slice(h * H_HEAD, (h + 1) * H_HEAD)
            q = _silu(proj_ref[rows, h * H_HEAD:(h + 1) * H_HEAD])
            fg = forget[:, hc]
            kk = 1.0 - fg
            v_bf = proj_ref[rows, v_off + h * H_HEAD:v_off + (h + 1) * H_HEAD].astype(BF16)
            p = p_all[:, hc]
            s_in = st_ref[h]
            q_bf = q.astype(BF16)
            kk_bf = kk.astype(BF16)
            a = jnp.where(row == col, _dot_nt(q_bf, kk_bf), 0.0)
            for (lg, right, sign, pair) in levels:
                if lg == 0:
                    e_bf = jnp.where(right, fg, 1.0).astype(BF16)
                else:
                    e_bf = jnp.exp2(_boundary_distance(p, 1 << lg, sign)).astype(BF16)
                a = jnp.where(pair, _dot_nt(q_bf * e_bf, kk_bf * e_bf), a)
            o = _dot(jnp.concatenate([a.astype(BF16), (q * jnp.exp2(p)).astype(BF16)], axis=1),
                     jnp.concatenate([v_bf, s_in.astype(BF16)], axis=0))
            p_last = p[CHUNK - 1:CHUNK, :]
            k_end = kk * jnp.exp2(p_last - p)
            decay_col = jnp.broadcast_to(jnp.exp2(p_last), (CHUNK, H_HEAD)).T
            st_ref[h] = decay_col * s_in + _dot(k_end.T.astype(BF16), v_bf)
            o = o * lax.rsqrt(jnp.mean(o * o, axis=-1, keepdims=True) + EPS) * hnw_ref[...]
            o = o * _silu(proj_ref[rows, g_off + h * H_HEAD:g_off + (h + 1) * H_HEAD])
            obf_ref[rows, hc] = o.astype(BF16)

    def step(cur_ref, nxt_ref):
        u = _rms(xin_ref[...], nw_ref[...]).astype(BF16)
        for c in range(n_chunks):
            chunk_body(c, cur_ref, nxt_ref, u)
        o_ref[...] = xres_ref[...] + _dot(obf_ref[...], wout_ref[...])

    @pl.when(lax.rem(n, 2) == 0)
    def _():
        step(proj_b, proj_a)

    @pl.when(lax.rem(n, 2) == 1)
    def _():
        step(proj_a, proj_b)


def _hgrn(x, nw, w_in, w_out, lb_logits, hnw, *, which, layer, tl=512):
    b, l, d = x.shape
    depth, d_key = lb_logits.shape
    d_proj = w_in.shape[-1]
    d_val = w_out.shape[-2]
    n_heads = d_key // H_HEAD
    tiles_per_seq = l // tl
    n_tiles = b * tiles_per_seq
    xt = x.reshape(n_tiles, tl, d)
    out = pl.pallas_call(
        functools.partial(_hgrn_kernel, tl=tl, layer=layer, tiles_per_seq=tiles_per_seq),
        out_shape=jax.ShapeDtypeStruct((n_tiles, tl, d), F32),
        grid=(n_tiles + 1,),
        in_specs=[
            pl.BlockSpec((None, tl, d), lambda n: (jnp.minimum(n, n_tiles - 1), 0, 0)),
            pl.BlockSpec((None, tl, d), lambda n: (jnp.maximum(n - 1, 0), 0, 0)),
            _const_spec((1, d)),
            _const_spec((d, d_proj), (which,)),
            _const_spec((depth, d_key)),
            _const_spec((1, H_HEAD)),
            _const_spec((d_val, d), (which,)),
        ],
        out_specs=pl.BlockSpec((None, tl, d), lambda n: (jnp.maximum(n - 1, 0), 0, 0)),
        scratch_shapes=[
            pltpu.VMEM((tl, d_proj), F32),
            pltpu.VMEM((tl, d_proj), F32),
            pltpu.VMEM((tl, d_val), BF16),
            pltpu.VMEM((n_heads, H_HEAD, H_HEAD), F32),
        ],
        compiler_params=pltpu.CompilerParams(
            dimension_semantics=("arbitrary",), vmem_limit_bytes=VMEM_LIMIT),
        name="hgrn",
    )(xt, xt, nw.reshape(1, d), w_in, lb_logits, hnw.reshape(1, H_HEAD), w_out)
    return out.reshape(b, l, d)


def kernel(x, norm_w, ffn_w_gate, ffn_w_up, ffn_w_down, m_w_in, m_conv_w, m_conv_b, m_dt_bias, m_a_log,
           m_d, m_norm_w, m_w_out, h_w_in, h_lb_logits, h_norm_w, h_w_out, final_norm_w):
    b, l, d = x.shape
    depth = norm_w.shape[0]
    fnw = final_norm_w.reshape(1, d)
    wg, wu, wd = ffn_w_gate.astype(BF16), ffn_w_up.astype(BF16), ffn_w_down.astype(BF16)
    m_in, m_out = m_w_in.astype(BF16), m_w_out.astype(BF16)
    h_in, h_out = h_w_in.astype(BF16), h_w_out.astype(BF16)

    def ffn(x, i, j, final=False):
        y = _ffn(x.reshape(b * l, d), norm_w[i, 2 * j].reshape(1, d), wg, wu, wd, fnw,
                 which=(i, j), final=final)
        return y.reshape(b, l, d)

    for i in range(depth):
        x = ffn(x, i, 0)
        j = i // 2
        if i % 2 == 0:
            x = _mamba(x, norm_w[i, 1], m_in, m_out, m_conv_w[j], m_conv_b[j], m_dt_bias[j], m_a_log[j],
                       m_d[j], m_norm_w[j], which=j)
        else:
            x = _hgrn(x, norm_w[i, 1], h_in, h_out, h_lb_logits, h_norm_w[j], which=j, layer=i)
        x = ffn(x, i, 1, final=(i == depth - 1))
    return x
```
